```python
import math
import jax, jax.numpy as jnp
from jax import lax
import numpy as np

D_MODEL = 1024
BATCH = 2
SEQ = 8192
DEPTH = 4

N_MIXERS = 4
HEAD_DIM = 64
N_HEADS = D_MODEL // HEAD_DIM
ROT_DIM = HEAD_DIM // 4
ROPE_THETA = 500000.0
Q_BLOCK = 128
DIL_CONFIGS = ((128, 1), (512, 4), (2048, 16))
N_HEADS_DIL = 8
DIL_GROUP_WIDTH = N_HEADS_DIL * HEAD_DIM
DIL_IN_WIDTH = len(DIL_CONFIGS) * 3 * DIL_GROUP_WIDTH
IDX_HEADS = 8
IDX_DIM = 64
TOPK_TOKENS = 256
DSA_IN_WIDTH = 3 * D_MODEL + IDX_HEADS * IDX_DIM + IDX_DIM + IDX_HEADS
MOBA_BLOCK = 256
MOBA_TOPK = 3
MOBA_Q_CHUNK = 32
D_FF = 4 * D_MODEL
PLE_DIM = 256
EPS = 1e-6

kernel_name = 'hybrid_sb_dilated_dsa_moba_trunk'


def _n_layers_of(mixer):
    return len(range(mixer, DEPTH, N_MIXERS))


def rms_norm(x, g):
    x32 = x.astype(jnp.float32)
    y = x32 * lax.rsqrt(jnp.mean(x32 * x32, axis=-1, keepdims=True) + EPS)
    return (y * g.astype(jnp.float32)).astype(x.dtype)


def rope_tables(positions):
    inv_freq = ROPE_THETA ** (-jnp.arange(0, ROT_DIM, 2, dtype=jnp.float32) / ROT_DIM)
    ang = positions.astype(jnp.float32)[..., None] * inv_freq
    return jnp.cos(ang)[:, :, None, :], jnp.sin(ang)[:, :, None, :]


def apply_partial_rope(t, cos, sin):
    half = ROT_DIM // 2
    t1 = t[..., :half].astype(jnp.float32)
    t2 = t[..., half:ROT_DIM].astype(jnp.float32)
    rot = jnp.concatenate([t1 * cos - t2 * sin, t2 * cos + t1 * sin], axis=-1).astype(t.dtype)
    return jnp.concatenate([rot, t[..., ROT_DIM:]], axis=-1)


def stick_breaking_attention(x, w_in, w_out):
    B, S, _ = x.shape
    q, k, v = jnp.split(x @ w_in, 3, axis=-1)
    q = q.reshape(B, S, N_HEADS, HEAD_DIM)
    k = k.reshape(B, S, N_HEADS, HEAD_DIM)
    v = v.reshape(B, S, N_HEADS, HEAD_DIM)
    scale = HEAD_DIM ** -0.5
    outs = []
    for blk in range(S // Q_BLOCK):
        t0 = blk * Q_BLOCK
        L = t0 + Q_BLOCK
        z = jnp.einsum('bqhd,bkhd->bhqk', q[:, t0:L], k[:, :L]).astype(jnp.float32) * scale
        t_idx = t0 + jnp.arange(Q_BLOCK)[:, None]
        s_idx = jnp.arange(L)[None, :]
        past = s_idx < t_idx
        log_fail = jnp.where(past, jax.nn.log_sigmoid(-z), 0.0)
        later = lax.cumsum(log_fail, axis=3, reverse=True) - log_fail
        a = jnp.where(past, jnp.exp(jax.nn.log_sigmoid(z) + later), 0.0)
        outs.append(jnp.einsum('bhqk,bkhd->bqhd', a.astype(v.dtype), v[:, :L]))
    o = jnp.concatenate(outs, axis=1).reshape(B, S, N_HEADS * HEAD_DIM)
    return o @ w_out


def banded_attention(q, k, v, n_back):
    N, L, H, Dh = q.shape
    nb = L // Q_BLOCK
    qb = q.reshape(N, nb, Q_BLOCK, H, Dh)
    kb = k.reshape(N, nb, Q_BLOCK, H, Dh)
    vb = v.reshape(N, nb, Q_BLOCK, H, Dh)
    zero = jnp.zeros_like(kb[:, :1])
    k2 = jnp.concatenate([jnp.concatenate([zero, kb[:, :-1]], axis=1), kb], axis=2)
    v2 = jnp.concatenate([jnp.concatenate([zero, vb[:, :-1]], axis=1), vb], axis=2)
    s = jnp.einsum('nbqhd,nbkhd->nbhqk', qb, k2).astype(jnp.float32) * (Dh ** -0.5)
    dist = Q_BLOCK + jnp.arange(Q_BLOCK)[:, None] - jnp.arange(2 * Q_BLOCK)[None, :]
    band = (dist >= 0) & (dist <= n_back)
    first_pad = (jnp.arange(nb) == 0)[:, None, None] & (jnp.arange(2 * Q_BLOCK) < Q_BLOCK)[None, None, :]
    mask = band[None] & ~first_pad
    s = jnp.where(mask[None, :, None], s, -jnp.inf)
    lse = jax.nn.logsumexp(s, axis=-1)
    prob = jnp.exp(s - lse[..., None])
    o = jnp.einsum('nbhqk,nbkhd->nbqhd', prob.astype(v.dtype), v2)
    return o.reshape(N, L, H, Dh), lse.transpose(0, 1, 3, 2).reshape(N, L, H)


def dilated_attention(x, w_in, w_out, cos, sin):
    B, S, _ = x.shape
    H = N_HEADS_DIL
    G = len(DIL_CONFIGS)
    proj = (x @ w_in).reshape(B, S, G, 3, H, HEAD_DIM)
    outs, lses = [], []
    for g, (window, dil) in enumerate(DIL_CONFIGS):
        q = apply_partial_rope(proj[:, :, g, 0], cos, sin)
        k = apply_partial_rope(proj[:, :, g, 1], cos, sin)
        v = proj[:, :, g, 2]
        L = S // dil
        Lp = -(-L // Q_BLOCK) * Q_BLOCK

        def strided(t):
            t = t.reshape(B, L, dil, H, HEAD_DIM).transpose(0, 2, 1, 3, 4).reshape(B * dil, L, H, HEAD_DIM)
            return jnp.pad(t, ((0, 0), (0, Lp - L), (0, 0), (0, 0)))

        o, lse = banded_attention(strided(q), strided(k), strided(v), window // dil)
        o = o[:, :L].reshape(B, dil, L, H, HEAD_DIM).transpose(0, 2, 1, 3, 4).reshape(B, S, H, HEAD_DIM)
        lse = lse[:, :L].reshape(B, dil, L, H).transpose(0, 2, 1, 3).reshape(B, S, H)
        outs.append(o)
        lses.append(lse)
    alpha = jax.nn.softmax(jnp.stack(lses, axis=0), axis=0)
    o = jnp.einsum('gbsh,gbshd->bshd', alpha, jnp.stack(outs, axis=0).astype(jnp.float32)).astype(x.dtype)
    return o.reshape(B, S, H * HEAD_DIM) @ w_out


def dsa_attention(x, w_in, w_out, cos, sin):
    B, S, _ = x.shape
    D = N_HEADS * HEAD_DIM
    cuts = [D, 2 * D, 3 * D, 3 * D + IDX_HEADS * IDX_DIM, 3 * D + IDX_HEADS * IDX_DIM + IDX_DIM]
    q, k, v, qi, ki, wi = jnp.split(x @ w_in, cuts, axis=-1)
    q = apply_partial_rope(q.reshape(B, S, N_HEADS, HEAD_DIM), cos, sin)
    k = apply_partial_rope(k.reshape(B, S, N_HEADS, HEAD_DIM), cos, sin)
    v = v.reshape(B, S, N_HEADS, HEAD_DIM)
    qi = apply_partial_rope(qi.reshape(B, S, IDX_HEADS, IDX_DIM), cos, sin)
    ki = apply_partial_rope(ki.reshape(B, S, 1, IDX_DIM), cos, sin)[:, :, 0]
    wi = wi.astype(jnp.float32) * (IDX_HEADS ** -0.5)
    topk = min(TOPK_TOKENS, S // 4)
    scale = HEAD_DIM ** -0.5
    key_pos = jnp.arange(S)

    def block(bi):
        t0 = bi * Q_BLOCK
        t_pos = t0 + jnp.arange(Q_BLOCK)
        qb = lax.dynamic_slice_in_dim(q, t0, Q_BLOCK, axis=1)
        qib = lax.dynamic_slice_in_dim(qi, t0, Q_BLOCK, axis=1)
        wib = lax.dynamic_slice_in_dim(wi, t0, Q_BLOCK, axis=1)
        rel = jax.nn.relu(jnp.einsum('bqhd,bkd->bqhk', qib, ki).astype(jnp.float32))
        score = jnp.einsum('bqh,bqhk->bqk', wib, rel)
        causal = key_pos[None, :] <= t_pos[:, None]
        score = jnp.where(causal[None], score, -jnp.inf)
        _, idx = lax.top_k(score, topk)
        valid = idx <= t_pos[None, :, None]
        k_sel = jax.vmap(lambda kk, ii: kk[ii])(k, idx)
        v_sel = jax.vmap(lambda vv, ii: vv[ii])(v, idx)
        s = jnp.einsum('bqhd,bqkhd->bqhk', qb, k_sel).astype(jnp.float32) * scale
        s = jnp.where(valid[:, :, None, :], s, -jnp.inf)
        prob = jax.nn.softmax(s, axis=-1)
        return jnp.einsum('bqhk,bqkhd->bqhd', prob.astype(v.dtype), v_sel)

    o = lax.map(block, jnp.arange(S // Q_BLOCK))
    o = o.transpose(1, 0, 2, 3, 4).reshape(B, S, D)
    return o @ w_out


def moba_attention(x, w_in, w_out, cos, sin):
    B, S, _ = x.shape
    H = N_HEADS
    q, k, v = jnp.split(x @ w_in, 3, axis=-1)
    q = apply_partial_rope(q.reshape(B, S, H, HEAD_DIM), cos, sin)
    k = apply_partial_rope(k.reshape(B, S, H, HEAD_DIM), cos, sin)
    v = v.reshape(B, S, H, HEAD_DIM)
    nb = -(-S // MOBA_BLOCK)
    Sp = nb * MOBA_BLOCK
    pad = ((0, 0), (0, Sp - S), (0, 0), (0, 0))
    kb = jnp.pad(k, pad).reshape(B, nb, MOBA_BLOCK, H, HEAD_DIM).transpose(0, 3, 1, 2, 4)
    vb = jnp.pad(v, pad).reshape(B, nb, MOBA_BLOCK, H, HEAD_DIM).transpose(0, 3, 1, 2, 4)
    k_mean = jnp.mean(kb.astype(jnp.float32), axis=3)
    topk = min(MOBA_TOPK, nb - 1)
    scale = HEAD_DIM ** -0.5
    b_ix = jnp.arange(B)[:, None, None, None]
    h_ix = jnp.arange(H)[None, None, :, None]

    def chunk(ci):
        t0 = ci * MOBA_Q_CHUNK
        t_pos = t0 + jnp.arange(MOBA_Q_CHUNK)
        cur = t0 // MOBA_BLOCK
        qc = lax.dynamic_slice_in_dim(q, t0, MOBA_Q_CHUNK, axis=1)
        k_own = lax.dynamic_index_in_dim(kb, cur, axis=2, keepdims=False)
        v_own = lax.dynamic_index_in_dim(vb, cur, axis=2, keepdims=False)
        own_pos = cur * MOBA_BLOCK + jnp.arange(MOBA_BLOCK)
        s_own = jnp.einsum('bqhd,bhkd->bqhk', qc, k_own).astype(jnp.float32) * scale
        s_own = jnp.where((own_pos[None, :] <= t_pos[:, None])[None, :, None, :], s_own, -jnp.inf)
        if topk == 0:
            prob = jax.nn.softmax(s_own, axis=-1)
            return jnp.einsum('bqhk,bhkd->bqhd', prob.astype(v.dtype), v_own)
        gate = jnp.einsum('bqhd,bhnd->bqhn', qc.astype(jnp.float32), k_mean)
        gate = jnp.where(jnp.arange(nb) < cur, gate, -jnp.inf)
        _, sel = lax.top_k(gate, topk)
        valid = sel < cur
        k_sel = kb[b_ix, h_ix, sel]
        v_sel = vb[b_ix, h_ix, sel]
        s_sel = jnp.einsum('bqhd,bqhjkd->bqhjk', qc, k_sel).astype(jnp.float32) * scale
        s_sel = jnp.where(valid[..., None], s_sel, -jnp.inf).reshape(B, MOBA_Q_CHUNK, H, topk * MOBA_BLOCK)
        prob = jax.nn.softmax(jnp.concatenate([s_sel, s_own], axis=-1), axis=-1).astype(v.dtype)
        p_sel = prob[..., :topk * MOBA_BLOCK].reshape(B, MOBA_Q_CHUNK, H, topk, MOBA_BLOCK)
        p_own = prob[..., topk * MOBA_BLOCK:]
        return (jnp.einsum('bqhjk,bqhjkd->bqhd', p_sel, v_sel)
                + jnp.einsum('bqhk,bhkd->bqhd', p_own, v_own))

    o = lax.map(chunk, jnp.arange(S // MOBA_Q_CHUNK))
    o = o.transpose(1, 0, 2, 3, 4).reshape(B, S, H * HEAD_DIM)
    return o @ w_out


def setup_inputs(seed: int = 0) -> dict:
    key = jax.random.key(seed)
    ks = iter(jax.random.split(key, 32))

    def dense(shape, fan_in):
        return jax.random.normal(next(ks), shape, jnp.float32) * (fan_in ** -0.5)

    def gain(shape):
        return 1.0 + 0.05 * jax.random.normal(next(ks), shape, jnp.float32)

    nA, nB, nC, nD = (_n_layers_of(m) for m in range(N_MIXERS))
    x = jax.random.normal(next(ks), (BATCH, SEQ, D_MODEL), jnp.float32)
    p = jax.random.normal(next(ks), (DEPTH, BATCH, SEQ, PLE_DIM), jnp.float32)
    offsets = jax.random.randint(next(ks), (BATCH, 1), 0, 4096, dtype=jnp.int32)
    positions = offsets + jnp.arange(SEQ, dtype=jnp.int32)[None, :]
    return {
        'x': x,
        'p': p,
        'positions': positions,
        'w_in_sb': dense((nA, D_MODEL, 3 * D_MODEL), D_MODEL),
        'w_out_sb': dense((nA, D_MODEL, D_MODEL), D_MODEL),
        'w_in_dil': dense((nB, D_MODEL, DIL_IN_WIDTH), D_MODEL),
        'w_out_dil': dense((nB, DIL_GROUP_WIDTH, D_MODEL), DIL_GROUP_WIDTH),
        'w_in_dsa': dense((nC, D_MODEL, DSA_IN_WIDTH), D_MODEL),
        'w_out_dsa': dense((nC, D_MODEL, D_MODEL), D_MODEL),
        'w_in_moba': dense((nD, D_MODEL, 3 * D_MODEL), D_MODEL),
        'w_out_moba': dense((nD, D_MODEL, D_MODEL), D_MODEL),
        'g_mix_pre': gain((DEPTH, D_MODEL)),
        'g_mix_post': gain((DEPTH, D_MODEL)),
        'g_ffn_pre': gain((DEPTH, D_MODEL)),
        'g_ffn_post': gain((DEPTH, D_MODEL)),
        'w_ff_in': dense((DEPTH, D_MODEL, D_FF), D_MODEL),
        'w_ff_out': dense((DEPTH, D_FF, D_MODEL), D_FF),
        'g_ple': gain((DEPTH, D_MODEL)),
        'w_ple_gate': dense((DEPTH, D_MODEL, D_MODEL), D_MODEL),
        'w_ple': dense((DEPTH, PLE_DIM, D_MODEL), PLE_DIM),
    }


def reference(x, p, positions, w_in_sb, w_out_sb, w_in_dil, w_out_dil, w_in_dsa, w_out_dsa,
              w_in_moba, w_out_moba, g_mix_pre, g_mix_post, g_ffn_pre, g_ffn_post,
              w_ff_in, w_ff_out, g_ple, w_ple_gate, w_ple):
    cos, sin = rope_tables(positions)
    h = x
    for i in range(DEPTH):
        mixer, j = i % N_MIXERS, i // N_MIXERS
        u = rms_norm(h, g_mix_pre[i])
        if mixer == 0:
            y = stick_breaking_attention(u, w_in_sb[j], w_out_sb[j])
        elif mixer == 1:
            y = dilated_attention(u, w_in_dil[j], w_out_dil[j], cos, sin)
        elif mixer == 2:
            y = dsa_attention(u, w_in_dsa[j], w_out_dsa[j], cos, sin)
        else:
            y = moba_attention(u, w_in_moba[j], w_out_moba[j], cos, sin)
        h = h + rms_norm(y, g_mix_post[i])
        u = rms_norm(h, g_ffn_pre[i])
        f = jnp.square(jax.nn.relu(u @ w_ff_in[i])) @ w_ff_out[i]
        h = h + rms_norm(f, g_ffn_post[i])
        gate = jax.nn.sigmoid(rms_norm(h, g_ple[i]) @ w_ple_gate[i])
        h = h + (p[i] @ w_ple[i]) * gate
    return h
```

```python
import functools

import jax
import jax.numpy as jnp
import numpy as np
from jax import lax
from jax.experimental import pallas as pl
from jax.experimental.pallas import tpu as pltpu

F32 = jnp.float32
BF16 = jnp.bfloat16
I32 = jnp.int32

LANES = 128
HEAD_DIM = 64
HALF = HEAD_DIM
ROT_DIM = HEAD_DIM // 4
ROPE_THETA = 500000.0
EPS = 1e-6
NEG = -1e30
SCALE = HEAD_DIM ** -0.5
DIL_CONFIGS = ((128, 1), (512, 4), (2048, 16))
BAND = 128
IDX_HEADS = 8
TOPK_TOKENS = 256
MOBA_BLOCK = 256
MOBA_TOPK = 3
INT_MIN = -2 ** 31
VMEM_LIMIT = 52 * 1024 * 1024


def _cparams(*sem):
    return pltpu.CompilerParams(dimension_semantics=sem, vmem_limit_bytes=VMEM_LIMIT)


def _rms(x, g):
    return x * lax.rsqrt(jnp.mean(x * x, axis=-1, keepdims=True) + EPS) * g


def _dot(a, b):
    return jnp.dot(a, b, preferred_element_type=F32)


def _dot_nt(a, b):
    return lax.dot_general(a, b, (((1,), (1,)), ((), ())), preferred_element_type=F32)


def _split3(a):
    hi = a.astype(BF16)
    lo = (a - hi.astype(F32)).astype(BF16)
    return hi, lo


def _dot_precise(a, b):
    a_hi, a_lo = _split3(a)
    b_hi, b_lo = _split3(b)
    return _dot(a_hi, b_hi) + _dot(a_hi, b_lo) + _dot(a_lo, b_hi)


def _lane_lt_half(shape):
    return lax.broadcasted_iota(I32, shape, 1) < HALF


def _proj_kernel(*refs, rope_fn, tn):
    if rope_fn is None:
        x_ref, g_ref, w_ref, o_ref, xn_ref = refs
    else:
        x_ref, g_ref, w_ref, c_ref, sa_ref, sb_ref, o_ref, xn_ref = refs
    j = pl.program_id(1)

    @pl.when(j == 0)
    def _():
        xn_ref[...] = _rms(x_ref[...], g_ref[...]).astype(BF16)

    y = _dot(xn_ref[...], w_ref[...])
    if rope_fn is None:
        o_ref[...] = y.astype(o_ref.dtype)
        return
    cond = rope_fn(j)

    @pl.when(cond)
    def _():
        c = c_ref[...]
        sa = sa_ref[...]
        sb = sb_ref[...]
        for u in range(tn // LANES):
            yu = y[:, u * LANES:(u + 1) * LANES]
            r = yu * c + pltpu.roll(yu, LANES - ROT_DIM // 2, 1) * sa + pltpu.roll(yu, ROT_DIM // 2, 1) * sb
            o_ref[:, u * LANES:(u + 1) * LANES] = r.astype(o_ref.dtype)

    @pl.when(jnp.logical_not(cond))
    def _():
        o_ref[...] = y.astype(o_ref.dtype)


def _project(h, g, w, out_dtype, rope=None, rope_fn=None, tm=1024, tn=512):
    n, d = h.shape
    nout = w.shape[1]
    in_specs = [
        pl.BlockSpec((tm, d), lambda i, j: (i, 0)),
        pl.BlockSpec((1, d), lambda i, j: (0, 0)),
        pl.BlockSpec((d, tn), lambda i, j: (0, j)),
    ]
    args = [h, g.reshape(1, d), w]
    if rope_fn is not None:
        in_specs += [pl.BlockSpec((tm, LANES), lambda i, j: (i, 0))] * 3
        args += list(rope)
    return pl.pallas_call(
        functools.partial(_proj_kernel, rope_fn=rope_fn, tn=tn),
        grid=(n // tm, nout // tn),
        in_specs=in_specs,
        out_specs=pl.BlockSpec((tm, tn), lambda i, j: (i, j)),
        out_shape=jax.ShapeDtypeStruct((n, nout), out_dtype),
        scratch_shapes=[pltpu.VMEM((tm, d), BF16)],
        compiler_params=_cparams("parallel", "arbitrary"),
        name="proj",
    )(*args)


def _outproj_kernel(o_ref, w_ref, g_ref, h_ref, out_ref):
    y = _dot(o_ref[...], w_ref[...])
    out_ref[...] = h_ref[...] + _rms(y, g_ref[...])


def _outproj(o, w, g, h, tm=512):
    n, d = h.shape
    k = o.shape[1]
    return pl.pallas_call(
        _outproj_kernel,
        grid=(n // tm,),
        in_specs=[
            pl.BlockSpec((tm, k), lambda i: (i, 0)),
            pl.BlockSpec((k, d), lambda i: (0, 0)),
            pl.BlockSpec((1, d), lambda i: (0, 0)),
            pl.BlockSpec((tm, d), lambda i: (i, 0)),
        ],
        out_specs=pl.BlockSpec((tm, d), lambda i: (i, 0)),
        out_shape=jax.ShapeDtypeStruct((n, d), F32),
        compiler_params=_cparams("parallel"),
        name="outproj",
    )(o, w, g.reshape(1, d), h)


def _dil_outproj_kernel(o0_ref, o1_ref, o2_ref, l0_ref, l1_ref, l2_ref, w_ref, g_ref, h_ref, out_ref):
    l0, l1, l2 = l0_ref[...], l1_ref[...], l2_ref[...]
    m = jnp.maximum(jnp.maximum(l0, l1), l2)
    e0, e1, e2 = jnp.exp(l0 - m), jnp.exp(l1 - m), jnp.exp(l2 - m)
    o = (e0 * o0_ref[...] + e1 * o1_ref[...] + e2 * o2_ref[...]) / (e0 + e1 + e2)
    y = _dot(o.astype(BF16), w_ref[...])
    out_ref[...] = h_ref[...] + _rms(y, g_ref[...])


def _dil_outproj(os_, ls_, w, g, h, tm=512):
    n, d = h.shape
    k = w.shape[0]
    tile = pl.BlockSpec((tm, k), lambda i: (i, 0))
    return pl.pallas_call(
        _dil_outproj_kernel,
        grid=(n // tm,),
        in_specs=[tile] * 6 + [
            pl.BlockSpec((k, d), lambda i: (0, 0)),
            pl.BlockSpec((1, d), lambda i: (0, 0)),
            pl.BlockSpec((tm, d), lambda i: (i, 0)),
        ],
        out_specs=pl.BlockSpec((tm, d), lambda i: (i, 0)),
        out_shape=jax.ShapeDtypeStruct((n, d), F32),
        compiler_params=_cparams("parallel"),
        name="dil_outproj",
    )(*os_, *ls_, w, g.reshape(1, d), h)


def _ffn_kernel(h_ref, gpre_ref, w1_ref, w2_ref, gpost_ref, gple_ref, wg_ref, p_ref, wp_ref,
                out_ref, un_ref, acc_ref, *, nf):
    f = pl.program_id(1)

    @pl.when(f == 0)
    def _():
        un_ref[...] = _rms(h_ref[...], gpre_ref[...]).astype(BF16)
        acc_ref[...] = jnp.zeros_like(acc_ref)

    a = _dot(un_ref[...], w1_ref[...])
    a = jnp.square(jnp.maximum(a, 0.0)).astype(BF16)
    acc_ref[...] += _dot(a, w2_ref[...])

    @pl.when(f == nf - 1)
    def _():
        h = h_ref[...] + _rms(acc_ref[...], gpost_ref[...])
        u = _rms(h, gple_ref[...]).astype(BF16)
        gate = 1.0 / (1.0 + jnp.exp(-_dot(u, wg_ref[...])))
        e = _dot(p_ref[...].astype(BF16), wp_ref[...])
        out_ref[...] = h + e * gate


def _ffn_ple(h, gpre, w1, w2, gpost, gple, wg, p, wp, tm=1024, tf=512):
    n, d = h.shape
    dff = w1.shape[1]
    pd = p.shape[1]
    nf = dff // tf
    row = lambda i, f: (i, 0)
    const = lambda i, f: (0, 0)
    return pl.pallas_call(
        functools.partial(_ffn_kernel, nf=nf),
        grid=(n // tm, nf),
        in_specs=[
            pl.BlockSpec((tm, d), row),
            pl.BlockSpec((1, d), const),
            pl.BlockSpec((d, tf), lambda i, f: (0, f)),
            pl.BlockSpec((tf, d), lambda i, f: (f, 0)),
            pl.BlockSpec((1, d), const),
            pl.BlockSpec((1, d), const),
            pl.BlockSpec((d, d), const),
            pl.BlockSpec((tm, pd), row),
            pl.BlockSpec((pd, d), const),
        ],
        out_specs=pl.BlockSpec((tm, d), row),
        out_shape=jax.ShapeDtypeStruct((n, d), F32),
        scratch_shapes=[pltpu.VMEM((tm, d), BF16), pltpu.VMEM((tm, d), F32)],
        compiler_params=_cparams("parallel", "arbitrary"),
        name="ffn_ple",
    )(h, gpre.reshape(1, d), w1, w2, gpost.reshape(1, d), gple.reshape(1, d), wg, p, wp)


def _pair_tables(nq, last_kt_of, descending):
    qi, kt, first, last = [], [], [], []
    for i in range(nq):
        kts = list(range(last_kt_of(i) + 1))
        if descending:
            kts = kts[::-1]
        for n, t in enumerate(kts):
            qi.append(i)
            kt.append(t)
            first.append(1 if n == 0 else 0)
            last.append(1 if n == len(kts) - 1 else 0)
    mk = lambda v: jnp.asarray(np.asarray(v, dtype=np.int32))
    return mk(qi), mk(kt), mk(first), mk(last)


def _sb_kernel(qi_ref, kt_ref, first_ref, last_ref, q_ref, k_ref, v_ref, o_ref, acc_ref, carry_ref, *, tq, tk):
    s = pl.program_id(1)
    n_pairs = q_ref.shape[1] // LANES
    row = lax.broadcasted_iota(I32, (tq, tk), 0)
    col = lax.broadcasted_iota(I32, (tq, tk), 1)
    past = col < row
    u = jnp.where(lax.broadcasted_iota(I32, (tk, tk), 0) > lax.broadcasted_iota(I32, (tk, tk), 1), 1.0, 0.0).astype(BF16)
    lo_half = _lane_lt_half((tq, LANES))

    def sweep(diag):
        for hp in range(n_pairs):
            sl = slice(hp * LANES, (hp + 1) * LANES)
            qp = q_ref[:, sl] * SCALE
            kp = k_ref[:, sl]
            vp = v_ref[:, sl]
            pv = []
            for half in range(2):
                h = 2 * hp + half
                qm = jnp.where(lo_half if half == 0 else jnp.logical_not(lo_half), qp, jnp.zeros_like(qp))
                z = _dot_nt(qm, kp)
                sp = jnp.maximum(z, 0.0) + jnp.log(1.0 + jnp.exp(-jnp.abs(z)))
                lf = -sp
                if diag:
                    lf = jnp.where(past, lf, 0.0)
                hi, lo = _split3(lf)
                c = carry_ref[h]
                later = _dot(hi, u) + _dot(lo, u) + jnp.concatenate([c] * (tk // LANES), axis=1)
                a = jnp.exp(z - sp + later)
                if diag:
                    a = jnp.where(past, a, 0.0)
                pv.append(_dot(a.astype(BF16), vp))
                carry_ref[h] = c + jnp.sum(lf, axis=1, keepdims=True)
            acc_ref[:, sl] += jnp.where(lo_half, pv[0], pv[1])

    @pl.when(first_ref[s] == 1)
    def _():
        acc_ref[...] = jnp.zeros_like(acc_ref)
        carry_ref[...] = jnp.zeros_like(carry_ref)
        sweep(True)

    @pl.when(first_ref[s] == 0)
    def _():
        sweep(False)

    @pl.when(last_ref[s] == 1)
    def _():
        o_ref[...] = acc_ref[...].astype(o_ref.dtype)


def _sb_attention(qkv, b, s, d, tq=256):
    nq = s // tq
    tabs = _pair_tables(nq, lambda i: i, descending=True)
    npairs = int(tabs[0].shape[0])
    n_heads = d // HEAD_DIM
    grid_spec = pltpu.PrefetchScalarGridSpec(
        num_scalar_prefetch=4,
        grid=(b, npairs),
        in_specs=[
            pl.BlockSpec((None, tq, d), lambda bb, ss, qi, kt, fi, la: (bb, qi[ss], 0)),
            pl.BlockSpec((None, tq, d), lambda bb, ss, qi, kt, fi, la: (bb, kt[ss], 1)),
            pl.BlockSpec((None, tq, d), lambda bb, ss, qi, kt, fi, la: (bb, kt[ss], 2)),
        ],
        out_specs=pl.BlockSpec((None, tq, d), lambda bb, ss, qi, kt, fi, la: (bb, qi[ss], 0)),
        scratch_shapes=[pltpu.VMEM((tq, d), F32), pltpu.VMEM((n_heads, tq, LANES), F32)],
    )
    return pl.pallas_call(
        functools.partial(_sb_kernel, tq=tq, tk=tq),
        grid_spec=grid_spec,
        out_shape=jax.ShapeDtypeStruct((b, s, d), BF16),
        compiler_params=_cparams("parallel", "arbitrary"),
        name="sb_attn",
    )(*tabs, qkv, qkv, qkv)


def _banded_kernel(q_ref, kp_ref, kc_ref, vp_ref, vc_ref, o_ref, lse_ref):
    i = pl.program_id(2)
    tq = q_ref.shape[0]
    n_pairs = q_ref.shape[1] // LANES
    row = lax.broadcasted_iota(I32, (tq, 2 * tq), 0)
    col = lax.broadcasted_iota(I32, (tq, 2 * tq), 1)
    lo_col = jnp.where(i > 0, row, jnp.maximum(row, tq))
    bias = jnp.where(col >= lo_col, jnp.where(col <= row + tq, 0.0, NEG), NEG)
    lo_half = _lane_lt_half((tq, LANES))
    for hp in range(n_pairs):
        sl = slice(hp * LANES, (hp + 1) * LANES)
        qp = q_ref[:, sl] * SCALE
        k2 = jnp.concatenate([kp_ref[:, sl], kc_ref[:, sl]], axis=0)
        v2 = jnp.concatenate([vp_ref[:, sl], vc_ref[:, sl]], axis=0)
        outs, lses = [], []
        for half in range(2):
            qm = jnp.where(lo_half if half == 0 else jnp.logical_not(lo_half), qp, jnp.zeros_like(qp))
            sc = _dot_nt(qm, k2) + bias
            m = jnp.max(sc, axis=1, keepdims=True)
            p = jnp.exp(sc - m)
            l = jnp.sum(p, axis=1, keepdims=True)
            outs.append(_dot(p.astype(BF16), v2) / l)
            lses.append(jnp.broadcast_to(m + jnp.log(l), (tq, LANES)))
        o_ref[:, sl] = jnp.where(lo_half, outs[0], outs[1])
        lse_ref[:, sl] = jnp.where(lo_half, lses[0], lses[1])


def _banded_attention(proj, b, s, g, dil, gw):
    ncol = proj.shape[-1] // gw
    l = s // dil
    view = proj.reshape(b, l, dil * ncol * gw)
    nblk = l // BAND
    q_map = lambda bb, r, i: (bb, i, r * ncol + 3 * g)
    kc_map = lambda bb, r, i: (bb, i, r * ncol + 3 * g + 1)
    kp_map = lambda bb, r, i: (bb, jnp.maximum(i - 1, 0), r * ncol + 3 * g + 1)
    vc_map = lambda bb, r, i: (bb, i, r * ncol + 3 * g + 2)
    vp_map = lambda bb, r, i: (bb, jnp.maximum(i - 1, 0), r * ncol + 3 * g + 2)
    blk = lambda m: pl.BlockSpec((None, BAND, gw), m)
    out_map = lambda bb, r, i: (bb, i, r)
    o, lse = pl.pallas_call(
        _banded_kernel,
        grid=(b, dil, nblk),
        in_specs=[blk(q_map), blk(kp_map), blk(kc_map), blk(vp_map), blk(vc_map)],
        out_specs=[blk(out_map), blk(out_map)],
        out_shape=[jax.ShapeDtypeStruct((b, l, dil * gw), F32)] * 2,
        compiler_params=_cparams("parallel", "parallel", "arbitrary"),
        name="banded_attn",
    )(view, view, view, view, view)
    return o.reshape(b * s, gw), lse.reshape(b * s, gw)


def _flash_sweep(q_ref, k_ref, v_ref, m_ref, l_ref, acc_ref, bias_fn, tq):
    n_pairs = q_ref.shape[1] // LANES
    lo_half = _lane_lt_half((tq, LANES))
    for hp in range(n_pairs):
        sl = slice(hp * LANES, (hp + 1) * LANES)
        qp = (q_ref[:, sl] * SCALE).astype(BF16)
        kp = k_ref[:, sl].astype(BF16)
        vp = v_ref[:, sl].astype(BF16)
        alphas, pvs = [], []
        for half in range(2):
            h = 2 * hp + half
            qm = jnp.where(lo_half if half == 0 else jnp.logical_not(lo_half), qp, jnp.zeros_like(qp))
            sc = _dot_nt(qm, kp) + bias_fn(h)
            m_prev = m_ref[h]
            m_new = jnp.maximum(m_prev, jnp.max(sc, axis=1, keepdims=True))
            p = jnp.exp(sc - m_new[:, :1])
            alpha = jnp.exp(m_prev - m_new)
            l_ref[h] = alpha * l_ref[h] + jnp.sum(p, axis=1, keepdims=True)
            m_ref[h] = m_new
            alphas.append(alpha)
            pvs.append(_dot(p.astype(BF16), vp))
        acc_ref[:, sl] = acc_ref[:, sl] * jnp.where(lo_half, alphas[0], alphas[1]) + jnp.where(lo_half, pvs[0], pvs[1])


def _flash_init(m_ref, l_ref, acc_ref):
    m_ref[...] = jnp.full_like(m_ref, NEG)
    l_ref[...] = jnp.zeros_like(l_ref)
    acc_ref[...] = jnp.zeros_like(acc_ref)


def _flash_finish(o_ref, l_ref, acc_ref, tq):
    lo_half = _lane_lt_half((tq, LANES))
    for hp in range(o_ref.shape[1] // LANES):
        sl = slice(hp * LANES, (hp + 1) * LANES)
        l = jnp.where(lo_half, l_ref[2 * hp], l_ref[2 * hp + 1])
        o_ref[:, sl] = (acc_ref[:, sl] / l).astype(o_ref.dtype)


def _dsa_index_kernel(qi_ref, wi_ref, ki_ref, out_ref, key_ref, *, tq, tk, topk):
    i = pl.program_id(1)
    n_chunks_total = out_ref.shape[0]
    nch = (i * tq + tq + tk - 1) // tk
    row_g = i * tq + lax.broadcasted_iota(I32, (tq, tk), 0)
    col_l = lax.broadcasted_iota(I32, (tq, tk), 1)
    lo_half = _lane_lt_half((tq, LANES))
    wi = wi_ref[...] * (IDX_HEADS ** -0.5)

    def score_chunk(c, carry):
        ks = ki_ref[pl.ds(pl.multiple_of(c * tk, tk), tk), :].astype(BF16)
        score = jnp.zeros((tq, tk), F32)
        for hp in range(IDX_HEADS // 2):
            qp = qi_ref[:, hp * LANES:(hp + 1) * LANES].astype(BF16)
            for half in range(2):
                h = 2 * hp + half
                qm = jnp.where(lo_half if half == 0 else jnp.logical_not(lo_half), qp, jnp.zeros_like(qp))
                rel = jnp.maximum(_dot_nt(qm, ks), 0.0)
                score = score + wi[:, h:h + 1] * rel
        score = jnp.where(c * tk + col_l <= row_g, score, -jnp.inf) + 0.0
        bits = lax.bitcast_convert_type(score, I32)
        key_ref[c] = jnp.where(bits < 0, bits ^ 0x7FFFFFFF, bits)
        return carry

    lax.fori_loop(0, nch, score_chunk, 0)

    def count(pred_fn):
        def body(c, acc):
            keys = key_ref[c]
            for u in range(tk // LANES):
                acc = acc + jnp.where(pred_fn(keys[:, u * LANES:(u + 1) * LANES]), 1, 0)
            return acc
        acc = lax.fori_loop(0, nch, body, jnp.zeros((tq, LANES), I32))
        return jnp.sum(acc, axis=1, keepdims=True)

    def bit_step(b, t_u):
        cand_u = t_u | lax.shift_left(jnp.int32(1), 31 - b)
        cand = jnp.broadcast_to(cand_u ^ INT_MIN, (tq, LANES))
        cnt = count(lambda kk: kk >= cand)
        return jnp.where(cnt >= topk, cand_u, t_u)

    t_u = lax.fori_loop(0, 32, bit_step, jnp.zeros((tq, 1), I32))
    thr = t_u ^ INT_MIN
    thr_l = jnp.broadcast_to(thr, (tq, LANES))
    need = (topk - count(lambda kk: kk > thr_l)).astype(F32)
    lt = jnp.where(lax.broadcasted_iota(I32, (tk, tk), 0) < lax.broadcasted_iota(I32, (tk, tk), 1), 1.0, 0.0).astype(BF16)

    def mask_chunk(c, seen):
        keys = key_ref[c]
        eq = jnp.where(keys == thr, 1.0, 0.0)
        rank = _dot(eq.astype(BF16), lt) + seen
        take = jnp.where(keys > thr, 1.0, jnp.where(rank < need, eq, 0.0))
        take = jnp.where(c * tk + col_l <= row_g, take, 0.0)
        out_ref[c] = jnp.where(take > 0.5, 0.0, NEG).astype(out_ref.dtype)
        return seen + jnp.sum(eq, axis=1, keepdims=True)

    lax.fori_loop(0, nch, mask_chunk, jnp.zeros((tq, 1), F32))

    def fill_chunk(c, carry):
        out_ref[c] = jnp.full((tq, tk), NEG, out_ref.dtype)
        return carry

    lax.fori_loop(nch, n_chunks_total, fill_chunk, 0)


def _dsa_index(idx, b, s, tq=128, tk=512):
    nq, nk = s // tq, s // tk
    return pl.pallas_call(
        functools.partial(_dsa_index_kernel, tq=tq, tk=tk, topk=min(TOPK_TOKENS, s // 4)),
        grid=(b, nq),
        in_specs=[
            pl.BlockSpec((None, tq, 4 * LANES), lambda bb, i: (bb, i, 0)),
            pl.BlockSpec((None, tq, LANES), lambda bb, i: (bb, i, 5)),
            pl.BlockSpec((None, s, LANES), lambda bb, i: (bb, 0, 4)),
        ],
        out_specs=pl.BlockSpec((None, None, nk, tq, tk), lambda bb, i: (bb, i, 0, 0, 0)),
        out_shape=jax.ShapeDtypeStruct((b, nq, nk, tq, tk), BF16),
        scratch_shapes=[pltpu.VMEM((nk, tq, tk), I32)],
        compiler_params=_cparams("parallel", "arbitrary"),
        name="dsa_index",
    )(idx, idx, idx)


def _dsa_attn_kernel(qi_ref, kt_ref, first_ref, last_ref, q_ref, k_ref, v_ref, bias_ref, o_ref,
                     m_ref, l_ref, acc_ref, *, tq, tk):
    s = pl.program_id(1)

    @pl.when(first_ref[s] == 1)
    def _():
        _flash_init(m_ref, l_ref, acc_ref)

    bias = bias_ref[...].reshape(tq, tk).astype(F32)
    _flash_sweep(q_ref, k_ref, v_ref, m_ref, l_ref, acc_ref, lambda h: bias, tq)

    @pl.when(last_ref[s] == 1)
    def _():
        _flash_finish(o_ref, l_ref, acc_ref, tq)


def _dsa_attention(qkv, bias, b, s, d, tq=256, tk=512, tq_idx=128):
    nq = s // tq
    sub = tq // tq_idx
    tabs = _pair_tables(nq, lambda i: (i * tq + tq - 1) // tk, descending=False)
    npairs = int(tabs[0].shape[0])
    n_heads = d // HEAD_DIM
    grid_spec = pltpu.PrefetchScalarGridSpec(
        num_scalar_prefetch=4,
        grid=(b, npairs),
        in_specs=[
            pl.BlockSpec((None, tq, d), lambda bb, ss, qi, kt, fi, la: (bb, qi[ss], 0)),
            pl.BlockSpec((None, tk, d), lambda bb, ss, qi, kt, fi, la: (bb, kt[ss], 1)),
            pl.BlockSpec((None, tk, d), lambda bb, ss, qi, kt, fi, la: (bb, kt[ss], 2)),
            pl.BlockSpec((None, sub, None, tq_idx, tk), lambda bb, ss, qi, kt, fi, la: (bb, qi[ss], kt[ss], 0, 0)),
        ],
        out_specs=pl.BlockSpec((None, tq, d), lambda bb, ss, qi, kt, fi, la: (bb, qi[ss], 0)),
        scratch_shapes=[pltpu.VMEM((n_heads, tq, LANES), F32), pltpu.VMEM((n_heads, tq, LANES), F32),
                        pltpu.VMEM((tq, d), F32)],
    )
    return pl.pallas_call(
        functools.partial(_dsa_attn_kernel, tq=tq, tk=tk),
        grid_spec=grid_spec,
        out_shape=jax.ShapeDtypeStruct((b, s, d), BF16),
        compiler_params=_cparams("parallel", "arbitrary"),
        name="dsa_attn",
    )(*tabs, qkv, qkv, qkv, bias)


def _block_mean_kernel(k_ref, o_ref):
    rows = k_ref.shape[0]
    d = k_ref.shape[1]
    o_ref[...] = jnp.sum(k_ref[...].reshape(rows // MOBA_BLOCK, MOBA_BLOCK, d), axis=1) * (1.0 / MOBA_BLOCK)


def _block_means(qkv, b, s, d, rows=2048):
    per = rows // MOBA_BLOCK
    return pl.pallas_call(
        _block_mean_kernel,
        grid=(b, s // rows),
        in_specs=[pl.BlockSpec((None, rows, d), lambda bb, i: (bb, i, 1))],
        out_specs=pl.BlockSpec((None, per, d), lambda bb, i: (bb, i, 0)),
        out_shape=jax.ShapeDtypeStruct((b, s // MOBA_BLOCK, d), F32),
        compiler_params=_cparams("parallel", "parallel"),
        name="moba_block_mean",
    )(qkv)


def _moba_select_kernel(q_ref, km_ref, o_ref, *, nb, n_heads, topk):
    cur = pl.program_id(1)
    tq = q_ref.shape[0]
    width = nb * n_heads
    gate = _dot_precise(q_ref[...], km_ref[...])
    blk = lax.broadcasted_iota(I32, (tq, width), 1) // n_heads
    gate = jnp.where(blk < cur, gate, -jnp.inf)
    rank = jnp.zeros((tq, width), F32)
    for r in range(1, nb):
        other = pltpu.roll(gate, r * n_heads, 1)
        tie_first = jnp.where(blk >= r, 1.0, 0.0)
        rank = rank + jnp.where(other > gate, 1.0, jnp.where(other == gate, tie_first, 0.0))
    bias = jnp.where(blk < cur, jnp.where(rank < topk, 0.0, NEG), NEG)
    for n in range(nb):
        o_ref[n] = bias[:, n * n_heads:(n + 1) * n_heads]


def _moba_select(qkv, km, b, s, d):
    nb = s // MOBA_BLOCK
    n_heads = d // HEAD_DIM
    topk = min(MOBA_TOPK, nb - 1)
    return pl.pallas_call(
        functools.partial(_moba_select_kernel, nb=nb, n_heads=n_heads, topk=topk),
        grid=(b, nb),
        in_specs=[
            pl.BlockSpec((None, MOBA_BLOCK, d), lambda bb, i: (bb, i, 0)),
            pl.BlockSpec((None, d, nb * n_heads), lambda bb, i: (bb, 0, 0)),
        ],
        out_specs=pl.BlockSpec((None, nb, MOBA_BLOCK, n_heads), lambda bb, i: (bb, 0, i, 0)),
        out_shape=jax.ShapeDtypeStruct((b, nb, s, n_heads), F32),
        compiler_params=_cparams("parallel", "parallel"),
        name="moba_select",
    )(qkv, km)


def _moba_attn_kernel(qi_ref, kt_ref, first_ref, last_ref, q_ref, k_ref, v_ref, sel_ref, o_ref,
                      m_ref, l_ref, acc_ref, *, tq):
    s = pl.program_id(1)

    @pl.when(first_ref[s] == 1)
    def _():
        _flash_init(m_ref, l_ref, acc_ref)

    @pl.when(last_ref[s] == 0)
    def _():
        sel = sel_ref[...]
        _flash_sweep(q_ref, k_ref, v_ref, m_ref, l_ref, acc_ref, lambda h: sel[:, h:h + 1], tq)

    @pl.when(last_ref[s] == 1)
    def _():
        row = lax.broadcasted_iota(I32, (tq, tq), 0)
        col = lax.broadcasted_iota(I32, (tq, tq), 1)
        causal = jnp.where(col <= row, 0.0, NEG)
        _flash_sweep(q_ref, k_ref, v_ref, m_ref, l_ref, acc_ref, lambda h: causal, tq)
        _flash_finish(o_ref, l_ref, acc_ref, tq)


def _moba_attention(qkv, sel, b, s, d):
    tq = MOBA_BLOCK
    nq = s // tq
    tabs = _pair_tables(nq, lambda i: i, descending=False)
    npairs = int(tabs[0].shape[0])
    n_heads = d // HEAD_DIM
    grid_spec = pltpu.PrefetchScalarGridSpec(
        num_scalar_prefetch=4,
        grid=(b, npairs),
        in_specs=[
            pl.BlockSpec((None, tq, d), lambda bb, ss, qi, kt, fi, la: (bb, qi[ss], 0)),
            pl.BlockSpec((None, tq, d), lambda bb, ss, qi, kt, fi, la: (bb, kt[ss], 1)),
            pl.BlockSpec((None, tq, d), lambda bb, ss, qi, kt, fi, la: (bb, kt[ss], 2)),
            pl.BlockSpec((None, None, tq, n_heads), lambda bb, ss, qi, kt, fi, la: (bb, kt[ss], qi[ss], 0)),
        ],
        out_specs=pl.BlockSpec((None, tq, d), lambda bb, ss, qi, kt, fi, la: (bb, qi[ss], 0)),
        scratch_shapes=[pltpu.VMEM((n_heads, tq, LANES), F32), pltpu.VMEM((n_heads, tq, LANES), F32),
                        pltpu.VMEM((tq, d), F32)],
    )
    return pl.pallas_call(
        functools.partial(_moba_attn_kernel, tq=tq),
        grid_spec=grid_spec,
        out_shape=jax.ShapeDtypeStruct((b, s, d), BF16),
        compiler_params=_cparams("parallel", "arbitrary"),
        name="moba_attn",
    )(*tabs, qkv, qkv, qkv, sel)


def _rope_lane_tables(positions):
    half = ROT_DIM // 2
    inv_freq = ROPE_THETA ** (-jnp.arange(0, ROT_DIM, 2, dtype=F32) / ROT_DIM)
    ang = positions.astype(F32).reshape(-1, 1) * inv_freq
    cos, sin = jnp.cos(ang), jnp.sin(ang)
    n = cos.shape[0]
    pad = HEAD_DIM - ROT_DIM
    c = jnp.concatenate([cos, cos, jnp.ones((n, pad), F32)], axis=1)
    sa = jnp.concatenate([-sin, jnp.zeros((n, half + pad), F32)], axis=1)
    sb = jnp.concatenate([jnp.zeros((n, half), F32), sin, jnp.zeros((n, pad), F32)], axis=1)
    rep = LANES // HEAD_DIM
    return tuple(jnp.tile(t, (1, rep)) for t in (c, sa, sb))


def _moba_gate_matrix(kmean, n_heads):
    b, nb, d = kmean.shape
    head_of_row = jnp.arange(d) // HEAD_DIM
    onehot = (head_of_row[:, None] == jnp.arange(n_heads)[None, :]).astype(F32)
    km = kmean.transpose(0, 2, 1)[:, :, :, None] * onehot[None, :, None, :]
    return km.reshape(b, d, nb * n_heads)


def kernel(x, p, positions, w_in_sb, w_out_sb, w_in_dil, w_out_dil, w_in_dsa, w_out_dsa, w_in_moba, w_out_moba,
           g_mix_pre, g_mix_post, g_ffn_pre, g_ffn_post, w_ff_in, w_ff_out, g_ple, w_ple_gate, w_ple):
    b, s, d = x.shape
    depth = p.shape[0]
    n = b * s
    n_heads = d // HEAD_DIM
    rope = _rope_lane_tables(positions)
    h = x.reshape(n, d)
    for i in range(depth):
        mixer, j = i % 4, i // 4
        if mixer == 0:
            qkv = _project(h, g_mix_pre[i], w_in_sb[j].astype(BF16), BF16)
            o = _sb_attention(qkv.reshape(b, s, 3 * d), b, s, d).reshape(n, d)
            h = _outproj(o, w_out_sb[j].astype(BF16), g_mix_post[i], h)
        elif mixer == 1:
            gw = w_out_dil.shape[1]
            proj = _project(h, g_mix_pre[i], w_in_dil[j].astype(BF16), BF16, rope, lambda jj: jj % 3 != 2, tn=gw)
            proj = proj.reshape(b, s, -1)
            outs, lses = zip(*[_banded_attention(proj, b, s, g, dil, gw) for g, (_, dil) in enumerate(DIL_CONFIGS)])
            h = _dil_outproj(outs, lses, w_out_dil[j].astype(BF16), g_mix_post[i], h)
        elif mixer == 2:
            w = w_in_dsa[j]
            w_qkv = w[:, :3 * d].astype(BF16)
            qi_w = w[:, 3 * d:3 * d + IDX_HEADS * HEAD_DIM]
            ki_w = w[:, 3 * d + IDX_HEADS * HEAD_DIM:3 * d + (IDX_HEADS + 1) * HEAD_DIM]
            wi_w = w[:, 3 * d + (IDX_HEADS + 1) * HEAD_DIM:]
            w_idx = jnp.concatenate([qi_w, ki_w, ki_w, wi_w, jnp.zeros((d, LANES - IDX_HEADS), F32)], axis=1).astype(BF16)
            qkv = _project(h, g_mix_pre[i], w_qkv, BF16, rope, lambda jj: jj < 4)
            idx = _project(h, g_mix_pre[i], w_idx, F32, rope, lambda jj: jj < 5, tn=LANES)
            bias = _dsa_index(idx.reshape(b, s, -1), b, s)
            o = _dsa_attention(qkv.reshape(b, s, 3 * d), bias, b, s, d).reshape(n, d)
            h = _outproj(o, w_out_dsa[j].astype(BF16), g_mix_post[i], h)
        else:
            qkv = _project(h, g_mix_pre[i], w_in_moba[j].astype(BF16), F32, rope, lambda jj: jj < 4)
            qkv = qkv.reshape(b, s, 3 * d)
            km = _moba_gate_matrix(_block_means(qkv, b, s, d), n_heads)
            sel = _moba_select(qkv, km, b, s, d)
            o = _moba_attention(qkv, sel, b, s, d).reshape(n, d)
            h = _outproj(o, w_out_moba[j].astype(BF16), g_mix_post[i], h)
        h = _ffn_ple(h, g_ffn_pre[i], w_ff_in[i].astype(BF16), w_ff_out[i].astype(BF16), g_ffn_post[i],
                     g_ple[i], w_ple_gate[i].astype(BF16), p[i].reshape(n, -1), w_ple[i].astype(BF16))
    return h.reshape(b, s, d)
```

```python
import functools

import jax
import jax.numpy as jnp
import numpy as np
from jax import lax
from jax.experimental import pallas as pl
from jax.experimental.pallas import tpu as pltpu

F32 = jnp.float32
BF16 = jnp.bfloat16
I32 = jnp.int32

LANES = 128
SUB = 8
SUB_BF16 = 16
HEAD_DIM = 64
HALF = HEAD_DIM
ROT_DIM = HEAD_DIM // 4
ROPE_THETA = 500000.0
EPS = 1e-6
NEG = -1e30
SCALE = HEAD_DIM ** -0.5
DIL_CONFIGS = ((128, 1), (512, 4), (2048, 16))
BAND = 128
IDX_HEADS = 8
TOPK_TOKENS = 256
MOBA_BLOCK = 256
MOBA_TOPK = 3
INT_MIN = -2 ** 31
VMEM_LIMIT = 52 * 1024 * 1024


def _cparams(*sem):
    return pltpu.CompilerParams(dimension_semantics=sem, vmem_limit_bytes=VMEM_LIMIT)


def _rms(x, g):
    return x * lax.rsqrt(jnp.mean(x * x, axis=-1, keepdims=True) + EPS) * g


def _dot(a, b):
    return jnp.dot(a, b, preferred_element_type=F32)


def _dot_nt(a, b):
    return lax.dot_general(a, b, (((1,), (1,)), ((), ())), preferred_element_type=F32)


def _split3(a):
    hi = a.astype(BF16)
    lo = (a - hi.astype(F32)).astype(BF16)
    return hi, lo


def _dot_precise(a, b):
    a_hi, a_lo = _split3(a)
    b_hi, b_lo = _split3(b)
    return _dot(a_hi, b_hi) + _dot(a_hi, b_lo) + _dot(a_lo, b_hi)


def _lane_lt_half(shape):
    return lax.broadcasted_iota(I32, shape, 1) < HALF


def _proj_kernel(*refs, rope_fn, tn):
    if rope_fn is None:
        x_ref, g_ref, w_ref, o_ref, xn_ref = refs
    else:
        x_ref, g_ref, w_ref, c_ref, sa_ref, sb_ref, o_ref, xn_ref = refs
    j = pl.program_id(1)

    @pl.when(j == 0)
    def _():
        xn_ref[...] = _rms(x_ref[...], g_ref[...]).astype(BF16)

    y = _dot(xn_ref[...], w_ref[...])
    if rope_fn is None:
        o_ref[...] = y.astype(o_ref.dtype)
        return
    cond = rope_fn(j)

    @pl.when(cond)
    def _():
        c = c_ref[...]
        sa = sa_ref[...]
        sb = sb_ref[...]
        for u in range(tn // LANES):
            yu = y[:, u * LANES:(u + 1) * LANES]
            r = yu * c + pltpu.roll(yu, LANES - ROT_DIM // 2, 1) * sa + pltpu.roll(yu, ROT_DIM // 2, 1) * sb
            o_ref[:, u * LANES:(u + 1) * LANES] = r.astype(o_ref.dtype)

    @pl.when(jnp.logical_not(cond))
    def _():
        o_ref[...] = y.astype(o_ref.dtype)


def _project(h, g, w, out_dtype, rope=None, rope_fn=None, tm=1024, tn=512):
    n, d = h.shape
    nout = w.shape[1]
    in_specs = [
        pl.BlockSpec((tm, d), lambda i, j: (i, 0)),
        pl.BlockSpec((1, d), lambda i, j: (0, 0)),
        pl.BlockSpec((d, tn), lambda i, j: (0, j)),
    ]
    args = [h, g.reshape(1, d), w]
    if rope_fn is not None:
        in_specs += [pl.BlockSpec((tm, LANES), lambda i, j: (i, 0))] * 3
        args += list(rope)
    return pl.pallas_call(
        functools.partial(_proj_kernel, rope_fn=rope_fn, tn=tn),
        grid=(n // tm, nout // tn),
        in_specs=in_specs,
        out_specs=pl.BlockSpec((tm, tn), lambda i, j: (i, j)),
        out_shape=jax.ShapeDtypeStruct((n, nout), out_dtype),
        scratch_shapes=[pltpu.VMEM((tm, d), BF16)],
        compiler_params=_cparams("parallel", "arbitrary"),
        name="proj",
    )(*args)


def _outproj_kernel(o_ref, w_ref, g_ref, h_ref, out_ref):
    y = _dot(o_ref[...], w_ref[...])
    out_ref[...] = h_ref[...] + _rms(y, g_ref[...])


def _outproj(o, w, g, h, tm=512):
    n, d = h.shape
    k = o.shape[1]
    return pl.pallas_call(
        _outproj_kernel,
        grid=(n // tm,),
        in_specs=[
            pl.BlockSpec((tm, k), lambda i: (i, 0)),
            pl.BlockSpec((k, d), lambda i: (0, 0)),
            pl.BlockSpec((1, d), lambda i: (0, 0)),
            pl.BlockSpec((tm, d), lambda i: (i, 0)),
        ],
        out_specs=pl.BlockSpec((tm, d), lambda i: (i, 0)),
        out_shape=jax.ShapeDtypeStruct((n, d), F32),
        compiler_params=_cparams("parallel"),
        name="outproj",
    )(o, w, g.reshape(1, d), h)


def _dil_outproj_kernel(o0_ref, o1_ref, o2_ref, l0_ref, l1_ref, l2_ref, w_ref, g_ref, h_ref, out_ref):
    l0, l1, l2 = l0_ref[...], l1_ref[...], l2_ref[...]
    m = jnp.maximum(jnp.maximum(l0, l1), l2)
    e0, e1, e2 = jnp.exp(l0 - m), jnp.exp(l1 - m), jnp.exp(l2 - m)
    o = (e0 * o0_ref[...] + e1 * o1_ref[...] + e2 * o2_ref[...]) / (e0 + e1 + e2)
    y = _dot(o.astype(BF16), w_ref[...])
    out_ref[...] = h_ref[...] + _rms(y, g_ref[...])


def _dil_outproj(os_, ls_, w, g, h, tm=512):
    n, d = h.shape
    k = w.shape[0]
    tile = pl.BlockSpec((tm, k), lambda i: (i, 0))
    return pl.pallas_call(
        _dil_outproj_kernel,
        grid=(n // tm,),
        in_specs=[tile] * 6 + [
            pl.BlockSpec((k, d), lambda i: (0, 0)),
            pl.BlockSpec((1, d), lambda i: (0, 0)),
            pl.BlockSpec((tm, d), lambda i: (i, 0)),
        ],
        out_specs=pl.BlockSpec((tm, d), lambda i: (i, 0)),
        out_shape=jax.ShapeDtypeStruct((n, d), F32),
        compiler_params=_cparams("parallel"),
        name="dil_outproj",
    )(*os_, *ls_, w, g.reshape(1, d), h)


def _ffn_kernel(h_ref, gpre_ref, w1_ref, w2_ref, gpost_ref, gple_ref, wg_ref, p_ref, wp_ref,
                out_ref, un_ref, acc_ref, *, nf):
    f = pl.program_id(1)

    @pl.when(f == 0)
    def _():
        un_ref[...] = _rms(h_ref[...], gpre_ref[...]).astype(BF16)
        acc_ref[...] = jnp.zeros_like(acc_ref)

    a = _dot(un_ref[...], w1_ref[...])
    a = jnp.square(jnp.maximum(a, 0.0)).astype(BF16)
    acc_ref[...] += _dot(a, w2_ref[...])

    @pl.when(f == nf - 1)
    def _():
        h = h_ref[...] + _rms(acc_ref[...], gpost_ref[...])
        u = _rms(h, gple_ref[...]).astype(BF16)
        gate = 1.0 / (1.0 + jnp.exp(-_dot(u, wg_ref[...])))
        e = _dot(p_ref[...].astype(BF16), wp_ref[...])
        out_ref[...] = h + e * gate


def _ffn_ple(h, gpre, w1, w2, gpost, gple, wg, p, wp, tm=1024, tf=512):
    n, d = h.shape
    dff = w1.shape[1]
    pd = p.shape[1]
    nf = dff // tf
    row = lambda i, f: (i, 0)
    const = lambda i, f: (0, 0)
    return pl.pallas_call(
        functools.partial(_ffn_kernel, nf=nf),
        grid=(n // tm, nf),
        in_specs=[
            pl.BlockSpec((tm, d), row),
            pl.BlockSpec((1, d), const),
            pl.BlockSpec((d, tf), lambda i, f: (0, f)),
            pl.BlockSpec((tf, d), lambda i, f: (f, 0)),
            pl.BlockSpec((1, d), const),
            pl.BlockSpec((1, d), const),
            pl.BlockSpec((d, d), const),
            pl.BlockSpec((tm, pd), row),
            pl.BlockSpec((pd, d), const),
        ],
        out_specs=pl.BlockSpec((tm, d), row),
        out_shape=jax.ShapeDtypeStruct((n, d), F32),
        scratch_shapes=[pltpu.VMEM((tm, d), BF16), pltpu.VMEM((tm, d), F32)],
        compiler_params=_cparams("parallel", "arbitrary"),
        name="ffn_ple",
    )(h, gpre.reshape(1, d), w1, w2, gpost.reshape(1, d), gple.reshape(1, d), wg, p, wp)


def _pair_tables(nq, last_kt_of, descending):
    qi, kt, first, last = [], [], [], []
    for i in range(nq):
        kts = list(range(last_kt_of(i) + 1))
        if descending:
            kts = kts[::-1]
        for n, t in enumerate(kts):
            qi.append(i)
            kt.append(t)
            first.append(1 if n == 0 else 0)
            last.append(1 if n == len(kts) - 1 else 0)
    mk = lambda v: jnp.asarray(np.asarray(v, dtype=np.int32))
    return mk(qi), mk(kt), mk(first), mk(last)


def _sb_kernel(qi_ref, kt_ref, first_ref, last_ref, q_ref, k_ref, v_ref, o_ref, acc_ref, carry_ref, z_ref, *, tq, tk):
    s = pl.program_id(1)
    n_pairs = q_ref.shape[1] // LANES
    row = lax.broadcasted_iota(I32, (tq, tk), 0)
    col = lax.broadcasted_iota(I32, (tq, tk), 1)
    past = col < row
    u = jnp.where(lax.broadcasted_iota(I32, (tk, tk), 0) > lax.broadcasted_iota(I32, (tk, tk), 1), 1.0, 0.0).astype(BF16)
    lo_half = _lane_lt_half((tq, LANES))

    def logits(h):
        sl = slice((h // 2) * LANES, (h // 2 + 1) * LANES)
        qp = q_ref[:, sl] * SCALE
        qm = jnp.where(lo_half if h % 2 == 0 else jnp.logical_not(lo_half), qp, jnp.zeros_like(qp))
        return _dot_nt(qm, k_ref[:, sl])

    def sweep(diag):
        n_heads = 2 * n_pairs
        pv = [None] * n_heads

        def finish(h, zs, later):
            a = jnp.exp(zs - later)
            if diag:
                a = jnp.where(past, a, 0.0)
            sl = slice((h // 2) * LANES, (h // 2 + 1) * LANES)
            pv[h] = _dot(a.astype(BF16), v_ref[:, sl])
            if h % 2 == 1:
                acc_ref[:, sl] += jnp.where(lo_half, pv[h - 1], pv[h])

        z_ref[0] = logits(0)
        pending = None
        for h in range(n_heads):
            z = z_ref[h % 2]
            if h + 1 < n_heads:
                z_ref[(h + 1) % 2] = logits(h + 1)
            neg_abs = lax.bitcast_convert_type(lax.bitcast_convert_type(z, I32) | INT_MIN, F32)
            sp = jnp.maximum(z, 0.0) + jnp.log(1.0 + jnp.exp(neg_abs))
            if diag:
                sp = jnp.where(past, sp, 0.0)
            c = carry_ref[h]
            later = _dot(sp.astype(BF16), u) + jnp.concatenate([c] * (tk // LANES), axis=1)
            carry_ref[h] = c + jnp.sum(sp, axis=1, keepdims=True)
            if pending is not None:
                finish(*pending)
            pending = (h, z - sp, later)
        finish(*pending)

    @pl.when(first_ref[s] == 1)
    def _():
        acc_ref[...] = jnp.zeros_like(acc_ref)
        carry_ref[...] = jnp.zeros_like(carry_ref)
        sweep(True)

    @pl.when(first_ref[s] == 0)
    def _():
        sweep(False)

    @pl.when(last_ref[s] == 1)
    def _():
        o_ref[...] = acc_ref[...].astype(o_ref.dtype)


def _sb_attention(qkv, b, s, d, tq=256):
    nq = s // tq
    tabs = _pair_tables(nq, lambda i: i, descending=True)
    npairs = int(tabs[0].shape[0])
    n_heads = d // HEAD_DIM
    grid_spec = pltpu.PrefetchScalarGridSpec(
        num_scalar_prefetch=4,
        grid=(b, npairs),
        in_specs=[
            pl.BlockSpec((None, tq, d), lambda bb, ss, qi, kt, fi, la: (bb, qi[ss], 0)),
            pl.BlockSpec((None, tq, d), lambda bb, ss, qi, kt, fi, la: (bb, kt[ss], 1)),
            pl.BlockSpec((None, tq, d), lambda bb, ss, qi, kt, fi, la: (bb, kt[ss], 2)),
        ],
        out_specs=pl.BlockSpec((None, tq, d), lambda bb, ss, qi, kt, fi, la: (bb, qi[ss], 0)),
        scratch_shapes=[pltpu.VMEM((tq, d), F32), pltpu.VMEM((n_heads, tq, LANES), F32),
                        pltpu.VMEM((2, tq, tq), F32)],
    )
    return pl.pallas_call(
        functools.partial(_sb_kernel, tq=tq, tk=tq),
        grid_spec=grid_spec,
        out_shape=jax.ShapeDtypeStruct((b, s, d), BF16),
        compiler_params=_cparams("parallel", "arbitrary"),
        name="sb_attn",
    )(*tabs, qkv, qkv, qkv)


def _banded_kernel(q_ref, kp_ref, kc_ref, vp_ref, vc_ref, o_ref, lse_ref):
    i = pl.program_id(2)
    tq = q_ref.shape[0]
    n_pairs = q_ref.shape[1] // LANES
    row = lax.broadcasted_iota(I32, (tq, 2 * tq), 0)
    col = lax.broadcasted_iota(I32, (tq, 2 * tq), 1)
    lo_col = jnp.where(i > 0, row, jnp.maximum(row, tq))
    bias = jnp.where(col >= lo_col, jnp.where(col <= row + tq, 0.0, NEG), NEG)
    lo_half = _lane_lt_half((tq, LANES))
    for hp in range(n_pairs):
        sl = slice(hp * LANES, (hp + 1) * LANES)
        qp = q_ref[:, sl] * SCALE
        k2 = jnp.concatenate([kp_ref[:, sl], kc_ref[:, sl]], axis=0)
        v2 = jnp.concatenate([vp_ref[:, sl], vc_ref[:, sl]], axis=0)
        outs, lses = [], []
        for half in range(2):
            qm = jnp.where(lo_half if half == 0 else jnp.logical_not(lo_half), qp, jnp.zeros_like(qp))
            sc = _dot_nt(qm, k2) + bias
            m = jnp.max(sc, axis=1, keepdims=True)
            p = jnp.exp(sc - m)
            l = jnp.sum(p, axis=1, keepdims=True)
            outs.append(_dot(p.astype(BF16), v2) / l)
            lses.append(jnp.broadcast_to(m + jnp.log(l), (tq, LANES)))
        o_ref[:, sl] = jnp.where(lo_half, outs[0], outs[1])
        lse_ref[:, sl] = jnp.where(lo_half, lses[0], lses[1])


def _banded_attention(proj, b, s, g, dil, gw):
    ncol = proj.shape[-1] // gw
    l = s // dil
    view = proj.reshape(b, l, dil * ncol * gw)
    nblk = l // BAND
    q_map = lambda bb, r, i: (bb, i, r * ncol + 3 * g)
    kc_map = lambda bb, r, i: (bb, i, r * ncol + 3 * g + 1)
    kp_map = lambda bb, r, i: (bb, jnp.maximum(i - 1, 0), r * ncol + 3 * g + 1)
    vc_map = lambda bb, r, i: (bb, i, r * ncol + 3 * g + 2)
    vp_map = lambda bb, r, i: (bb, jnp.maximum(i - 1, 0), r * ncol + 3 * g + 2)
    blk = lambda m: pl.BlockSpec((None, BAND, gw), m)
    out_map = lambda bb, r, i: (bb, i, r)
    o, lse = pl.pallas_call(
        _banded_kernel,
        grid=(b, dil, nblk),
        in_specs=[blk(q_map), blk(kp_map), blk(kc_map), blk(vp_map), blk(vc_map)],
        out_specs=[blk(out_map), blk(out_map)],
        out_shape=[jax.ShapeDtypeStruct((b, l, dil * gw), F32)] * 2,
        compiler_params=_cparams("parallel", "parallel", "arbitrary"),
        name="banded_attn",
    )(view, view, view, view, view)
    return o.reshape(b * s, gw), lse.reshape(b * s, gw)


def _flash_sweep(qt_ref, k_ref, vt_ref, m_ref, l_ref, acc_ref, s_ref, bias_fn, tk, tq):
    n_heads = qt_ref.shape[0] // HEAD_DIM
    zeros_q = jnp.zeros((HEAD_DIM, tq), BF16)
    ones_v = jnp.ones((SUB_BF16, tk), BF16)
    groups = HEAD_DIM // SUB

    def logits(h):
        qh = (qt_ref[h * HEAD_DIM:(h + 1) * HEAD_DIM, :] * SCALE).astype(BF16)
        qm = jnp.concatenate([qh, zeros_q] if h % 2 == 0 else [zeros_q, qh], axis=0)
        return _dot(k_ref[:, (h // 2) * LANES:(h // 2 + 1) * LANES].astype(BF16), qm)

    s_ref[0] = logits(0)
    for h in range(n_heads):
        sc = s_ref[h % 2] + bias_fn(h)
        s_ref[h % 2] = sc
        if h + 1 < n_heads:
            s_ref[(h + 1) % 2] = logits(h + 1)
        rows = slice(h * HEAD_DIM, (h + 1) * HEAD_DIM)
        m_prev = m_ref[h]
        m_new = jnp.maximum(m_prev, jnp.max(jnp.max(sc.reshape(tk // SUB, SUB, tq), axis=0), axis=0, keepdims=True))
        p = jnp.exp(s_ref[h % 2].reshape(tk // SUB, SUB, tq) - m_new[None]).reshape(tk, tq).astype(BF16)
        alpha = jnp.exp(m_prev - m_new)
        va = jnp.concatenate([vt_ref[rows, :].astype(BF16), ones_v], axis=0)
        pv = _dot(va, p)
        l_ref[h] = alpha * l_ref[h] + pv[HEAD_DIM:HEAD_DIM + SUB]
        m_ref[h] = m_new
        acc = acc_ref[rows, :].reshape(groups, SUB, tq) * alpha[None] + pv[:HEAD_DIM].reshape(groups, SUB, tq)
        acc_ref[rows, :] = acc.reshape(HEAD_DIM, tq)


def _flash_init(m_ref, l_ref, acc_ref):
    m_ref[...] = jnp.full_like(m_ref, NEG)
    l_ref[...] = jnp.zeros_like(l_ref)
    acc_ref[...] = jnp.zeros_like(acc_ref)


def _flash_finish(o_ref, l_ref, acc_ref, tq):
    groups = HEAD_DIM // SUB
    for h in range(o_ref.shape[0] // HEAD_DIM):
        rows = slice(h * HEAD_DIM, (h + 1) * HEAD_DIM)
        o = acc_ref[rows, :].reshape(groups, SUB, tq) / l_ref[h][None]
        o_ref[rows, :] = o.reshape(HEAD_DIM, tq).astype(o_ref.dtype)


def _dsa_index_kernel(qit_ref, wit_ref, ki_ref, out_ref, key_ref, *, tq, tk, topk):
    i = pl.program_id(1)
    n_chunks_total = out_ref.shape[0]
    nch = (i * tq + tq + tk - 1) // tk
    q_pos = i * tq + lax.broadcasted_iota(I32, (tk, tq), 1)
    k_row = lax.broadcasted_iota(I32, (tk, tq), 0)
    wi = wit_ref[...] * (IDX_HEADS ** -0.5)
    zeros_q = jnp.zeros((HEAD_DIM, tq), BF16)

    def score_chunk(c, carry):
        ks = ki_ref[pl.ds(pl.multiple_of(c * tk, tk), tk), :].astype(BF16)
        score = jnp.zeros((tk, tq), F32)
        for h in range(IDX_HEADS):
            qh = qit_ref[h * HEAD_DIM:(h + 1) * HEAD_DIM, :].astype(BF16)
            rel = jnp.maximum(_dot(ks, jnp.concatenate([qh, zeros_q], axis=0)), 0.0)
            score = score + wi[h:h + 1, :] * rel
        score = jnp.where(c * tk + k_row <= q_pos, score, -jnp.inf) + 0.0
        bits = lax.bitcast_convert_type(score, I32)
        key_ref[c] = jnp.where(bits < 0, bits ^ 0x7FFFFFFF, bits)
        return carry

    lax.fori_loop(0, nch, score_chunk, 0)

    def count(pred_fn):
        def body(c, acc):
            keys = key_ref[c]
            for g in range(tk // SUB):
                acc = acc + jnp.where(pred_fn(keys[g * SUB:(g + 1) * SUB, :]), 1.0, 0.0)
            return acc
        acc = lax.fori_loop(0, nch, body, jnp.zeros((SUB, tq), F32))
        return jnp.sum(acc, axis=0, keepdims=True)

    def bit_step(b, t_u):
        cand_u = t_u | lax.shift_left(jnp.int32(1), 31 - b)
        cand = jnp.broadcast_to(cand_u ^ INT_MIN, (SUB, tq))
        cnt = count(lambda kk: kk >= cand)
        return jnp.where(cnt >= topk, cand_u, t_u)

    t_u = lax.fori_loop(0, 32, bit_step, jnp.zeros((1, tq), I32))
    thr = t_u ^ INT_MIN
    thr_s = jnp.broadcast_to(thr, (SUB, tq))
    need = topk - count(lambda kk: kk > thr_s)
    lt = jnp.where(lax.broadcasted_iota(I32, (tk, tk), 1) < lax.broadcasted_iota(I32, (tk, tk), 0), 1.0, 0.0).astype(BF16)

    def mask_chunk(c, seen):
        keys = key_ref[c]
        eq = jnp.where(keys == thr, 1.0, 0.0)
        rank = _dot(lt, eq.astype(BF16)) + seen
        take = jnp.where(keys > thr, 1.0, jnp.where(rank < need, eq, 0.0))
        take = jnp.where(c * tk + k_row <= q_pos, take, 0.0)
        out_ref[c] = jnp.where(take > 0.5, 0.0, NEG).astype(out_ref.dtype)
        return seen + jnp.sum(eq, axis=0, keepdims=True)

    lax.fori_loop(0, nch, mask_chunk, jnp.zeros((1, tq), F32))

    def fill_chunk(c, carry):
        out_ref[c] = jnp.full((tk, tq), NEG, out_ref.dtype)
        return carry

    lax.fori_loop(nch, n_chunks_total, fill_chunk, 0)


def _dsa_index(qit, wit, idx, b, s, tq=256, tk=512):
    nq, nk = s // tq, s // tk
    return pl.pallas_call(
        functools.partial(_dsa_index_kernel, tq=tq, tk=tk, topk=min(TOPK_TOKENS, s // 4)),
        grid=(b, nq),
        in_specs=[
            pl.BlockSpec((None, IDX_HEADS * HEAD_DIM, tq), lambda bb, i: (bb, 0, i)),
            pl.BlockSpec((None, IDX_HEADS, tq), lambda bb, i: (bb, 0, i)),
            pl.BlockSpec((None, s, LANES), lambda bb, i: (bb, 0, 4)),
        ],
        out_specs=pl.BlockSpec((None, None, nk, tk, tq), lambda bb, i: (bb, i, 0, 0, 0)),
        out_shape=jax.ShapeDtypeStruct((b, nq, nk, tk, tq), BF16),
        scratch_shapes=[pltpu.VMEM((nk, tk, tq), I32)],
        compiler_params=_cparams("parallel", "arbitrary"),
        name="dsa_index",
    )(qit, wit, idx)


def _dsa_attn_kernel(qi_ref, kt_ref, first_ref, last_ref, qt_ref, k_ref, vt_ref, bias_ref, o_ref,
                     m_ref, l_ref, acc_ref, s_ref, *, tq, tk):
    s = pl.program_id(1)

    @pl.when(first_ref[s] == 1)
    def _():
        _flash_init(m_ref, l_ref, acc_ref)

    bias = bias_ref[...].astype(F32)
    _flash_sweep(qt_ref, k_ref, vt_ref, m_ref, l_ref, acc_ref, s_ref, lambda h: bias, tk, tq)

    @pl.when(last_ref[s] == 1)
    def _():
        _flash_finish(o_ref, l_ref, acc_ref, tq)


def _flash_specs(b, s, d, tq, tk, tabs, bias_spec):
    n_heads = d // HEAD_DIM
    return pltpu.PrefetchScalarGridSpec(
        num_scalar_prefetch=4,
        grid=(b, int(tabs[0].shape[0])),
        in_specs=[
            pl.BlockSpec((None, d, tq), lambda bb, ss, qi, kt, fi, la: (bb, 0, qi[ss])),
            pl.BlockSpec((None, tk, d), lambda bb, ss, qi, kt, fi, la: (bb, kt[ss], 1)),
            pl.BlockSpec((None, d, tk), lambda bb, ss, qi, kt, fi, la: (bb, 0, kt[ss])),
            bias_spec,
        ],
        out_specs=pl.BlockSpec((None, d, tq), lambda bb, ss, qi, kt, fi, la: (bb, 0, qi[ss])),
        scratch_shapes=[pltpu.VMEM((n_heads, SUB, tq), F32), pltpu.VMEM((n_heads, SUB, tq), F32),
                        pltpu.VMEM((d, tq), F32), pltpu.VMEM((2, tk, tq), F32)],
    )


def _dsa_attention(qt, qkv, vt, bias, b, s, d, tq=256, tk=512):
    tabs = _pair_tables(s // tq, lambda i: (i * tq + tq - 1) // tk, descending=False)
    bias_spec = pl.BlockSpec((None, None, None, tk, tq), lambda bb, ss, qi, kt, fi, la: (bb, qi[ss], kt[ss], 0, 0))
    return pl.pallas_call(
        functools.partial(_dsa_attn_kernel, tq=tq, tk=tk),
        grid_spec=_flash_specs(b, s, d, tq, tk, tabs, bias_spec),
        out_shape=jax.ShapeDtypeStruct((b, d, s), BF16),
        compiler_params=_cparams("parallel", "arbitrary"),
        name="dsa_attn",
    )(*tabs, qt, qkv, vt, bias)


def _block_mean_kernel(k_ref, o_ref):
    rows = k_ref.shape[0]
    d = k_ref.shape[1]
    o_ref[...] = jnp.sum(k_ref[...].reshape(rows // MOBA_BLOCK, MOBA_BLOCK, d), axis=1) * (1.0 / MOBA_BLOCK)


def _block_means(qkv, b, s, d, rows=2048):
    per = rows // MOBA_BLOCK
    return pl.pallas_call(
        _block_mean_kernel,
        grid=(b, s // rows),
        in_specs=[pl.BlockSpec((None, rows, d), lambda bb, i: (bb, i, 1))],
        out_specs=pl.BlockSpec((None, per, d), lambda bb, i: (bb, i, 0)),
        out_shape=jax.ShapeDtypeStruct((b, s // MOBA_BLOCK, d), F32),
        compiler_params=_cparams("parallel", "parallel"),
        name="moba_block_mean",
    )(qkv)


def _moba_select_kernel(qt_ref, kmt_ref, o_ref, *, nb, n_heads, topk):
    cur = pl.program_id(1)
    tq = qt_ref.shape[1]
    width = nb * n_heads
    gate = _dot_precise(kmt_ref[...], qt_ref[...])
    blk = lax.broadcasted_iota(I32, (width, tq), 0) // n_heads
    gate = jnp.where(blk < cur, gate, -jnp.inf)
    rank = jnp.zeros((width, tq), F32)
    for r in range(1, nb):
        sh = r * n_heads
        other = jnp.concatenate([gate[width - sh:], gate[:width - sh]], axis=0)
        tie_first = jnp.where(blk >= r, 1.0, 0.0)
        rank = rank + jnp.where(other > gate, 1.0, jnp.where(other == gate, tie_first, 0.0))
    bias = jnp.where(blk < cur, jnp.where(rank < topk, 0.0, NEG), NEG)
    o_ref[...] = bias.reshape(nb, n_heads, tq)


def _moba_select(qt, kmt, b, s, d):
    nb = s // MOBA_BLOCK
    n_heads = d // HEAD_DIM
    topk = min(MOBA_TOPK, nb - 1)
    return pl.pallas_call(
        functools.partial(_moba_select_kernel, nb=nb, n_heads=n_heads, topk=topk),
        grid=(b, nb),
        in_specs=[
            pl.BlockSpec((None, d, MOBA_BLOCK), lambda bb, i: (bb, 0, i)),
            pl.BlockSpec((None, nb * n_heads, d), lambda bb, i: (bb, 0, 0)),
        ],
        out_specs=pl.BlockSpec((None, nb, n_heads, MOBA_BLOCK), lambda bb, i: (bb, 0, 0, i)),
        out_shape=jax.ShapeDtypeStruct((b, nb, n_heads, s), F32),
        compiler_params=_cparams("parallel", "parallel"),
        name="moba_select",
    )(qt, kmt)


def _moba_attn_kernel(qi_ref, kt_ref, first_ref, last_ref, qt_ref, k_ref, vt_ref, sel_ref, o_ref,
                      m_ref, l_ref, acc_ref, s_ref, *, tq):
    s = pl.program_id(1)

    @pl.when(first_ref[s] == 1)
    def _():
        _flash_init(m_ref, l_ref, acc_ref)

    @pl.when(last_ref[s] == 0)
    def _():
        sel = sel_ref[...]
        _flash_sweep(qt_ref, k_ref, vt_ref, m_ref, l_ref, acc_ref, s_ref, lambda h: sel[h:h + 1, :], tq, tq)

    @pl.when(last_ref[s] == 1)
    def _():
        k_row = lax.broadcasted_iota(I32, (tq, tq), 0)
        q_col = lax.broadcasted_iota(I32, (tq, tq), 1)
        causal = jnp.where(k_row <= q_col, 0.0, NEG)
        _flash_sweep(qt_ref, k_ref, vt_ref, m_ref, l_ref, acc_ref, s_ref, lambda h: causal, tq, tq)
        _flash_finish(o_ref, l_ref, acc_ref, tq)


def _moba_attention(qt, qkv, vt, sel, b, s, d):
    tq = MOBA_BLOCK
    tabs = _pair_tables(s // tq, lambda i: i, descending=False)
    n_heads = d // HEAD_DIM
    bias_spec = pl.BlockSpec((None, None, n_heads, tq), lambda bb, ss, qi, kt, fi, la: (bb, kt[ss], 0, qi[ss]))
    return pl.pallas_call(
        functools.partial(_moba_attn_kernel, tq=tq),
        grid_spec=_flash_specs(b, s, d, tq, tq, tabs, bias_spec),
        out_shape=jax.ShapeDtypeStruct((b, d, s), BF16),
        compiler_params=_cparams("parallel", "arbitrary"),
        name="moba_attn",
    )(*tabs, qt, qkv, vt, sel)


def _rope_lane_tables(positions):
    half = ROT_DIM // 2
    inv_freq = ROPE_THETA ** (-jnp.arange(0, ROT_DIM, 2, dtype=F32) / ROT_DIM)
    ang = positions.astype(F32).reshape(-1, 1) * inv_freq
    cos, sin = jnp.cos(ang), jnp.sin(ang)
    n = cos.shape[0]
    pad = HEAD_DIM - ROT_DIM
    c = jnp.concatenate([cos, cos, jnp.ones((n, pad), F32)], axis=1)
    sa = jnp.concatenate([-sin, jnp.zeros((n, half + pad), F32)], axis=1)
    sb = jnp.concatenate([jnp.zeros((n, half), F32), sin, jnp.zeros((n, pad), F32)], axis=1)
    rep = LANES // HEAD_DIM
    return tuple(jnp.tile(t, (1, rep)) for t in (c, sa, sb))


def _moba_gate_matrix(kmean, n_heads):
    b, nb, d = kmean.shape
    head_of_col = jnp.arange(d) // HEAD_DIM
    onehot = (jnp.arange(n_heads)[:, None] == head_of_col[None, :]).astype(F32)
    return (kmean[:, :, None, :] * onehot[None, None, :, :]).reshape(b, nb * n_heads, d)


def kernel(x, p, positions, w_in_sb, w_out_sb, w_in_dil, w_out_dil, w_in_dsa, w_out_dsa, w_in_moba, w_out_moba,
           g_mix_pre, g_mix_post, g_ffn_pre, g_ffn_post, w_ff_in, w_ff_out, g_ple, w_ple_gate, w_ple):
    b, s, d = x.shape
    depth = p.shape[0]
    n = b * s
    n_heads = d // HEAD_DIM
    rope = _rope_lane_tables(positions)
    h = x.reshape(n, d)
    for i in range(depth):
        mixer, j = i % 4, i // 4
        if mixer == 0:
            qkv = _project(h, g_mix_pre[i], w_in_sb[j].astype(BF16), BF16)
            o = _sb_attention(qkv.reshape(b, s, 3 * d), b, s, d).reshape(n, d)
            h = _outproj(o, w_out_sb[j].astype(BF16), g_mix_post[i], h)
        elif mixer == 1:
            gw = w_out_dil.shape[1]
            proj = _project(h, g_mix_pre[i], w_in_dil[j].astype(BF16), BF16, rope, lambda jj: jj % 3 != 2, tn=gw)
            proj = proj.reshape(b, s, -1)
            outs, lses = zip(*[_banded_attention(proj, b, s, g, dil, gw) for g, (_, dil) in enumerate(DIL_CONFIGS)])
            h = _dil_outproj(outs, lses, w_out_dil[j].astype(BF16), g_mix_post[i], h)
        elif mixer == 2:
            w = w_in_dsa[j]
            w_qkv = w[:, :3 * d].astype(BF16)
            qi_w = w[:, 3 * d:3 * d + IDX_HEADS * HEAD_DIM]
            ki_w = w[:, 3 * d + IDX_HEADS * HEAD_DIM:3 * d + (IDX_HEADS + 1) * HEAD_DIM]
            wi_w = w[:, 3 * d + (IDX_HEADS + 1) * HEAD_DIM:]
            w_idx = jnp.concatenate([qi_w, ki_w, ki_w, wi_w, jnp.zeros((d, LANES - IDX_HEADS), F32)], axis=1).astype(BF16)
            qkv = _project(h, g_mix_pre[i], w_qkv, BF16, rope, lambda jj: jj < 4).reshape(b, s, 3 * d)
            idx = _project(h, g_mix_pre[i], w_idx, F32, rope, lambda jj: jj < 5, tn=LANES).reshape(b, s, -1)
            tr = lambda t: t.transpose(0, 2, 1)
            n_qi = IDX_HEADS * HEAD_DIM
            bias = _dsa_index(tr(idx[:, :, :n_qi]), tr(idx[:, :, n_qi + LANES:n_qi + LANES + IDX_HEADS]), idx, b, s)
            ot = _dsa_attention(tr(qkv[:, :, :d]), qkv, tr(qkv[:, :, 2 * d:]), bias, b, s, d)
            h = _outproj(tr(ot).reshape(n, d), w_out_dsa[j].astype(BF16), g_mix_post[i], h)
        else:
            qkv = _project(h, g_mix_pre[i], w_in_moba[j].astype(BF16), F32, rope, lambda jj: jj < 4)
            qkv = qkv.reshape(b, s, 3 * d)
            tr = lambda t: t.transpose(0, 2, 1)
            qt = tr(qkv[:, :, :d])
            kmt = _moba_gate_matrix(_block_means(qkv, b, s, d), n_heads)
            sel = _moba_select(qt, kmt, b, s, d)
            ot = _moba_attention(qt, qkv, tr(qkv[:, :, 2 * d:]).astype(BF16), sel, b, s, d)
            h = _outproj(tr(ot).reshape(n, d), w_out_moba[j].astype(BF16), g_mix_post[i], h)
        h = _ffn_ple(h, g_ffn_pre[i], w_ff_in[i].astype(BF16), w_ff_out[i].astype(BF16), g_ffn_post[i],
                     g_ple[i], w_ple_gate[i].astype(BF16), p[i].reshape(n, -1), w_ple[i].astype(BF16))
    return h.reshape(b, s, d)
```

```python
import functools

import jax
import jax.numpy as jnp
import numpy as np
from jax import lax
from jax.experimental import pallas as pl
from jax.experimental.pallas import tpu as pltpu

F32 = jnp.float32
BF16 = jnp.bfloat16
I32 = jnp.int32

LANES = 128
SUB = 8
SUB_BF16 = 16
HEAD_DIM = 64
HALF = HEAD_DIM
ROT_DIM = HEAD_DIM // 4
ROPE_THETA = 500000.0
EPS = 1e-6
NEG = -1e30
SCALE = HEAD_DIM ** -0.5
DIL_CONFIGS = ((128, 1), (512, 4), (2048, 16))
BAND = 128
IDX_HEADS = 8
TOPK_TOKENS = 256
MOBA_BLOCK = 256
MOBA_TOPK = 3
INT_MIN = -2 ** 31
VMEM_LIMIT = 52 * 1024 * 1024


def _cparams(*sem):
    return pltpu.CompilerParams(dimension_semantics=sem, vmem_limit_bytes=VMEM_LIMIT)


def _rms(x, g):
    return x * lax.rsqrt(jnp.mean(x * x, axis=-1, keepdims=True) + EPS) * g


def _dot(a, b):
    return jnp.dot(a, b, preferred_element_type=F32)


def _dot_nt(a, b):
    return lax.dot_general(a, b, (((1,), (1,)), ((), ())), preferred_element_type=F32)


def _split3(a):
    hi = a.astype(BF16)
    lo = (a - hi.astype(F32)).astype(BF16)
    return hi, lo


def _dot_precise(a, b):
    a_hi, a_lo = _split3(a)
    b_hi, b_lo = _split3(b)
    return _dot(a_hi, b_hi) + _dot(a_hi, b_lo) + _dot(a_lo, b_hi)


def _lane_lt_half(shape):
    return lax.broadcasted_iota(I32, shape, 1) < HALF


def _proj_kernel(*refs, rope_cols, rope_all, tn, layout, dil):
    refs = list(refs)
    y_ref = refs.pop() if layout == "dilated" else None
    if rope_cols:
        x_ref, g_ref, w_ref, c_ref, sa_ref, sb_ref, o_ref, xn_ref = refs
    else:
        x_ref, g_ref, w_ref, o_ref, xn_ref = refs
    j = pl.program_id(1)
    tm = x_ref.shape[0]

    @pl.when(j == 0)
    def _():
        xn_ref[...] = _rms(x_ref[...], g_ref[...]).astype(BF16)

    y = _dot(xn_ref[...], w_ref[...])

    def emit(rope_chunks):
        for u in range(tn // LANES):
            lanes = slice(u * LANES, (u + 1) * LANES)
            yu = y[:, lanes]
            if u < rope_chunks:
                yu = (yu * c_ref[...] + pltpu.roll(yu, LANES - ROT_DIM // 2, 1) * sa_ref[...]
                      + pltpu.roll(yu, ROT_DIM // 2, 1) * sb_ref[...])
            if layout == "rows":
                o_ref[:, lanes] = yu.astype(o_ref.dtype)
            elif layout == "cols":
                o_ref[lanes, :] = yu.T.astype(o_ref.dtype)
            else:
                y_ref[u] = yu
        if layout == "dilated":
            for r in range(dil):
                for u in range(tn // LANES):
                    col = r * tn + u * LANES
                    o_ref[:, col:col + LANES] = y_ref[u, pl.ds(r, tm // dil, stride=dil), :].astype(o_ref.dtype)

    if rope_cols == 0:
        emit(0)
    elif layout == "dilated":
        emit(min(rope_cols, tn) // LANES)
    elif rope_all:
        emit(tn // LANES)
    else:
        rope_blocks = rope_cols // tn

        @pl.when(j < rope_blocks)
        def _():
            emit(tn // LANES)

        @pl.when(j >= rope_blocks)
        def _():
            emit(0)


def _project(h, g, w, out_dtype, rope=None, rope_cols=0, tm=1024, tn=512, layout="rows", dil=1):
    n, d = h.shape
    nout = w.shape[1]
    in_specs = [
        pl.BlockSpec((tm, d), lambda i, j: (i, 0)),
        pl.BlockSpec((1, d), lambda i, j: (0, 0)),
        pl.BlockSpec((d, tn), lambda i, j: (0, j)),
    ]
    args = [h, g.reshape(1, d), w]
    if rope_cols:
        assert rope_cols % LANES == 0 and (layout == "dilated" or rope_cols % tn == 0)
        in_specs += [pl.BlockSpec((tm, LANES), lambda i, j: (i, 0))] * 3
        args += list(rope)
    scratch = [pltpu.VMEM((tm, d), BF16)]
    if layout == "rows":
        out_spec = pl.BlockSpec((tm, tn), lambda i, j: (i, j))
        out_shape = (n, nout)
    elif layout == "cols":
        out_spec = pl.BlockSpec((tn, tm), lambda i, j: (j, i))
        out_shape = (nout, n)
    else:
        assert tn == nout and tm % (dil * SUB) == 0
        out_spec = pl.BlockSpec((tm // dil, dil * nout), lambda i, j: (i, 0))
        out_shape = (n // dil, dil * nout)
        scratch.append(pltpu.VMEM((tn // LANES, tm, LANES), F32))
    return pl.pallas_call(
        functools.partial(_proj_kernel, rope_cols=rope_cols, rope_all=rope_cols == nout, tn=tn, layout=layout, dil=dil),
        grid=(n // tm, nout // tn),
        in_specs=in_specs,
        out_specs=out_spec,
        out_shape=jax.ShapeDtypeStruct(out_shape, out_dtype),
        scratch_shapes=scratch,
        compiler_params=_cparams("parallel", "arbitrary"),
        name="proj",
    )(*args)


def _outproj_kernel(o_ref, w_ref, g_ref, h_ref, out_ref):
    y = _dot(o_ref[...], w_ref[...])
    out_ref[...] = h_ref[...] + _rms(y, g_ref[...])


def _outproj(o, w, g, h, tm=512):
    n, d = h.shape
    k = o.shape[1]
    return pl.pallas_call(
        _outproj_kernel,
        grid=(n // tm,),
        in_specs=[
            pl.BlockSpec((tm, k), lambda i: (i, 0)),
            pl.BlockSpec((k, d), lambda i: (0, 0)),
            pl.BlockSpec((1, d), lambda i: (0, 0)),
            pl.BlockSpec((tm, d), lambda i: (i, 0)),
        ],
        out_specs=pl.BlockSpec((tm, d), lambda i: (i, 0)),
        out_shape=jax.ShapeDtypeStruct((n, d), F32),
        compiler_params=_cparams("parallel"),
        name="outproj",
    )(o, w, g.reshape(1, d), h)


def _dil_outproj_kernel(*refs, dils):
    n_g = len(dils)
    o_refs, l_refs = refs[:n_g], refs[n_g:2 * n_g]
    w_ref, g_ref, h_ref, out_ref = refs[2 * n_g:2 * n_g + 4]
    scr = refs[2 * n_g + 4:]
    tm = h_ref.shape[0]
    gw = w_ref.shape[0]

    def natural(ref, scr_ref, dil):
        if dil == 1:
            return ref[...]
        for r in range(dil):
            for u in range(gw // LANES):
                col = r * gw + u * LANES
                scr_ref[u, pl.ds(r, tm // dil, stride=dil), :] = ref[:, col:col + LANES]
        return jnp.concatenate([scr_ref[u] for u in range(gw // LANES)], axis=1)

    os_ = [natural(o_refs[i], scr[i], dils[i]) for i in range(n_g)]
    ls_ = [natural(l_refs[i], scr[n_g + i], dils[i]) for i in range(n_g)]
    m = functools.reduce(jnp.maximum, ls_)
    es = [jnp.exp(l - m) for l in ls_]
    o = sum(e * o for e, o in zip(es, os_)) / sum(es)
    y = _dot(o.astype(BF16), w_ref[...])
    out_ref[...] = h_ref[...] + _rms(y, g_ref[...])


def _dil_outproj(os_, ls_, dils, w, g, h, tm=512):
    n, d = h.shape
    k = w.shape[0]
    tiles = [pl.BlockSpec((tm // dil, dil * k), lambda i: (i, 0)) for dil in dils]
    return pl.pallas_call(
        functools.partial(_dil_outproj_kernel, dils=tuple(dils)),
        grid=(n // tm,),
        in_specs=tiles + tiles + [
            pl.BlockSpec((k, d), lambda i: (0, 0)),
            pl.BlockSpec((1, d), lambda i: (0, 0)),
            pl.BlockSpec((tm, d), lambda i: (i, 0)),
        ],
        out_specs=pl.BlockSpec((tm, d), lambda i: (i, 0)),
        out_shape=jax.ShapeDtypeStruct((n, d), F32),
        scratch_shapes=[pltpu.VMEM((k // LANES, tm, LANES), F32)] * (2 * len(dils)),
        compiler_params=_cparams("parallel"),
        name="dil_outproj",
    )(*os_, *ls_, w, g.reshape(1, d), h)


def _ffn_kernel(h_ref, gpre_ref, w1_ref, w2_ref, gpost_ref, gple_ref, wg_ref, p_ref, wp_ref,
                out_ref, un_ref, acc_ref, *, nf):
    f = pl.program_id(1)

    @pl.when(f == 0)
    def _():
        un_ref[...] = _rms(h_ref[...], gpre_ref[...]).astype(BF16)
        acc_ref[...] = jnp.zeros_like(acc_ref)

    a = _dot(un_ref[...], w1_ref[...])
    a = jnp.square(jnp.maximum(a, 0.0)).astype(BF16)
    acc_ref[...] += _dot(a, w2_ref[...])

    @pl.when(f == nf - 1)
    def _():
        h = h_ref[...] + _rms(acc_ref[...], gpost_ref[...])
        u = _rms(h, gple_ref[...]).astype(BF16)
        gate = 1.0 / (1.0 + jnp.exp(-_dot(u, wg_ref[...])))
        e = _dot(p_ref[...].astype(BF16), wp_ref[...])
        out_ref[...] = h + e * gate


def _ffn_ple(h, gpre, w1, w2, gpost, gple, wg, p, wp, tm=1024, tf=512):
    n, d = h.shape
    dff = w1.shape[1]
    pd = p.shape[1]
    nf = dff // tf
    row = lambda i, f: (i, 0)
    const = lambda i, f: (0, 0)
    return pl.pallas_call(
        functools.partial(_ffn_kernel, nf=nf),
        grid=(n // tm, nf),
        in_specs=[
            pl.BlockSpec((tm, d), row),
            pl.BlockSpec((1, d), const),
            pl.BlockSpec((d, tf), lambda i, f: (0, f)),
            pl.BlockSpec((tf, d), lambda i, f: (f, 0)),
            pl.BlockSpec((1, d), const),
            pl.BlockSpec((1, d), const),
            pl.BlockSpec((d, d), const),
            pl.BlockSpec((tm, pd), row),
            pl.BlockSpec((pd, d), const),
        ],
        out_specs=pl.BlockSpec((tm, d), row),
        out_shape=jax.ShapeDtypeStruct((n, d), F32),
        scratch_shapes=[pltpu.VMEM((tm, d), BF16), pltpu.VMEM((tm, d), F32)],
        compiler_params=_cparams("parallel", "arbitrary"),
        name="ffn_ple",
    )(h, gpre.reshape(1, d), w1, w2, gpost.reshape(1, d), gple.reshape(1, d), wg, p, wp)


def _pair_tables(nq, last_kt_of, descending):
    qi, kt, first, last = [], [], [], []
    for i in range(nq):
        kts = list(range(last_kt_of(i) + 1))
        if descending:
            kts = kts[::-1]
        for n, t in enumerate(kts):
            qi.append(i)
            kt.append(t)
            first.append(1 if n == 0 else 0)
            last.append(1 if n == len(kts) - 1 else 0)
    mk = lambda v: jnp.asarray(np.asarray(v, dtype=np.int32))
    return mk(qi), mk(kt), mk(first), mk(last)


def _sb_kernel(qi_ref, kt_ref, first_ref, last_ref, q_ref, k_ref, v_ref, o_ref, acc_ref, carry_ref, z_ref, *, tq, tk):
    s = pl.program_id(1)
    n_pairs = q_ref.shape[1] // LANES
    row = lax.broadcasted_iota(I32, (tq, tk), 0)
    col = lax.broadcasted_iota(I32, (tq, tk), 1)
    past = col < row
    u = jnp.where(lax.broadcasted_iota(I32, (tk, tk), 0) > lax.broadcasted_iota(I32, (tk, tk), 1), 1.0, 0.0).astype(BF16)
    lo_half = _lane_lt_half((tq, LANES))

    def logits(h):
        sl = slice((h // 2) * LANES, (h // 2 + 1) * LANES)
        qp = q_ref[:, sl] * SCALE
        qm = jnp.where(lo_half if h % 2 == 0 else jnp.logical_not(lo_half), qp, jnp.zeros_like(qp))
        return _dot_nt(qm, k_ref[:, sl])

    def sweep(diag):
        n_heads = 2 * n_pairs
        pv = [None] * n_heads

        def finish(h, zs, later):
            a = jnp.exp(zs - later)
            if diag:
                a = jnp.where(past, a, 0.0)
            sl = slice((h // 2) * LANES, (h // 2 + 1) * LANES)
            pv[h] = _dot(a.astype(BF16), v_ref[:, sl])
            if h % 2 == 1:
                acc_ref[:, sl] += jnp.where(lo_half, pv[h - 1], pv[h])

        slots = z_ref.shape[0]
        for h in range(min(slots - 1, n_heads)):
            z_ref[h] = logits(h)
        pending = None
        for h in range(n_heads):
            z = z_ref[h % slots]
            if h + slots - 1 < n_heads:
                z_ref[(h + slots - 1) % slots] = logits(h + slots - 1)
            neg_abs = lax.bitcast_convert_type(lax.bitcast_convert_type(z, I32) | INT_MIN, F32)
            sp = jnp.maximum(z, 0.0) + jnp.log(1.0 + jnp.exp(neg_abs))
            if diag:
                sp = jnp.where(past, sp, 0.0)
            c = carry_ref[h]
            later = _dot(sp.astype(BF16), u) + jnp.concatenate([c] * (tk // LANES), axis=1)
            carry_ref[h] = c + jnp.sum(sp, axis=1, keepdims=True)
            if pending is not None:
                finish(*pending)
            pending = (h, z - sp, later)
        finish(*pending)

    @pl.when(first_ref[s] == 1)
    def _():
        acc_ref[...] = jnp.zeros_like(acc_ref)
        carry_ref[...] = jnp.zeros_like(carry_ref)
        sweep(True)

    @pl.when(first_ref[s] == 0)
    def _():
        sweep(False)

    @pl.when(last_ref[s] == 1)
    def _():
        o_ref[...] = acc_ref[...].astype(o_ref.dtype)


def _sb_attention(qkv, b, s, d, tq=256):
    nq = s // tq
    tabs = _pair_tables(nq, lambda i: i, descending=True)
    npairs = int(tabs[0].shape[0])
    n_heads = d // HEAD_DIM
    grid_spec = pltpu.PrefetchScalarGridSpec(
        num_scalar_prefetch=4,
        grid=(b, npairs),
        in_specs=[
            pl.BlockSpec((None, tq, d), lambda bb, ss, qi, kt, fi, la: (bb, qi[ss], 0)),
            pl.BlockSpec((None, tq, d), lambda bb, ss, qi, kt, fi, la: (bb, kt[ss], 1)),
            pl.BlockSpec((None, tq, d), lambda bb, ss, qi, kt, fi, la: (bb, kt[ss], 2)),
        ],
        out_specs=pl.BlockSpec((None, tq, d), lambda bb, ss, qi, kt, fi, la: (bb, qi[ss], 0)),
        scratch_shapes=[pltpu.VMEM((tq, d), F32), pltpu.VMEM((n_heads, tq, LANES), F32),
                        pltpu.VMEM((2, tq, tq), F32)],
    )
    return pl.pallas_call(
        functools.partial(_sb_kernel, tq=tq, tk=tq),
        grid_spec=grid_spec,
        out_shape=jax.ShapeDtypeStruct((b, s, d), BF16),
        compiler_params=_cparams("parallel", "arbitrary"),
        name="sb_attn",
    )(*tabs, qkv, qkv, qkv)


def _banded_kernel(q_ref, kp_ref, kc_ref, vp_ref, vc_ref, o_ref, lse_ref):
    i = pl.program_id(2)
    tq = q_ref.shape[0]
    n_pairs = q_ref.shape[1] // LANES
    row = lax.broadcasted_iota(I32, (tq, 2 * tq), 0)
    col = lax.broadcasted_iota(I32, (tq, 2 * tq), 1)
    lo_col = jnp.where(i > 0, row, jnp.maximum(row, tq))
    bias = jnp.where(col >= lo_col, jnp.where(col <= row + tq, 0.0, NEG), NEG)
    lo_half = _lane_lt_half((tq, LANES))
    for hp in range(n_pairs):
        sl = slice(hp * LANES, (hp + 1) * LANES)
        qp = q_ref[:, sl] * SCALE
        k2 = jnp.concatenate([kp_ref[:, sl], kc_ref[:, sl]], axis=0)
        v2 = jnp.concatenate([vp_ref[:, sl], vc_ref[:, sl]], axis=0)
        outs, lses = [], []
        for half in range(2):
            qm = jnp.where(lo_half if half == 0 else jnp.logical_not(lo_half), qp, jnp.zeros_like(qp))
            sc = _dot_nt(qm, k2) + bias
            m = jnp.max(sc, axis=1, keepdims=True)
            p = jnp.exp(sc - m)
            l = jnp.sum(p, axis=1, keepdims=True)
            outs.append(_dot(p.astype(BF16), v2) / l)
            lses.append(jnp.broadcast_to(m + jnp.log(l), (tq, LANES)))
        o_ref[:, sl] = jnp.where(lo_half, outs[0], outs[1])
        lse_ref[:, sl] = jnp.where(lo_half, lses[0], lses[1])


def _banded_attention(proj, b, s, dil, gw):
    ncol = 3
    l = s // dil
    view = proj.reshape(b, l, dil * ncol * gw)
    nblk = l // BAND
    q_map = lambda bb, r, i: (bb, i, r * ncol)
    kc_map = lambda bb, r, i: (bb, i, r * ncol + 1)
    kp_map = lambda bb, r, i: (bb, jnp.maximum(i - 1, 0), r * ncol + 1)
    vc_map = lambda bb, r, i: (bb, i, r * ncol + 2)
    vp_map = lambda bb, r, i: (bb, jnp.maximum(i - 1, 0), r * ncol + 2)
    blk = lambda m: pl.BlockSpec((None, BAND, gw), m)
    out_map = lambda bb, r, i: (bb, i, r)
    o, lse = pl.pallas_call(
        _banded_kernel,
        grid=(b, dil, nblk),
        in_specs=[blk(q_map), blk(kp_map), blk(kc_map), blk(vp_map), blk(vc_map)],
        out_specs=[blk(out_map), blk(out_map)],
        out_shape=[jax.ShapeDtypeStruct((b, l, dil * gw), F32)] * 2,
        compiler_params=_cparams("parallel", "parallel", "arbitrary"),
        name="banded_attn",
    )(view, view, view, view, view)
    return o.reshape(b * l, dil * gw), lse.reshape(b * l, dil * gw)


def _flash_sweep(qt_ref, k_ref, vt_ref, m_ref, l_ref, acc_ref, s_ref, bias_fn, tk, tq):
    n_heads = qt_ref.shape[0] // HEAD_DIM
    zeros_q = jnp.zeros((HEAD_DIM, tq), BF16)
    ones_v = jnp.ones((SUB_BF16, tk), BF16)
    groups = HEAD_DIM // SUB

    def logits(h):
        qh = (qt_ref[h * HEAD_DIM:(h + 1) * HEAD_DIM, :] * SCALE).astype(BF16)
        qm = jnp.concatenate([qh, zeros_q] if h % 2 == 0 else [zeros_q, qh], axis=0)
        return _dot(k_ref[:, (h // 2) * LANES:(h // 2 + 1) * LANES].astype(BF16), qm)

    slots = s_ref.shape[0]
    for h in range(min(slots - 1, n_heads)):
        s_ref[h] = logits(h)
    for h in range(n_heads):
        sc = (s_ref[h % slots] + bias_fn(h)).reshape(tk // SUB, SUB, tq)
        if h + slots - 1 < n_heads:
            s_ref[(h + slots - 1) % slots] = logits(h + slots - 1)
        rows = slice(h * HEAD_DIM, (h + 1) * HEAD_DIM)
        m_prev = m_ref[h]
        m_new = jnp.maximum(m_prev, jnp.max(jnp.max(sc, axis=0), axis=0, keepdims=True))
        p = jnp.exp(sc - m_new[None]).reshape(tk, tq).astype(BF16)
        alpha = jnp.exp(m_prev - m_new)
        va = jnp.concatenate([vt_ref[rows, :].astype(BF16), ones_v], axis=0)
        pv = _dot(va, p)
        l_ref[h] = alpha * l_ref[h] + pv[HEAD_DIM:HEAD_DIM + SUB]
        m_ref[h] = m_new
        acc = acc_ref[rows, :].reshape(groups, SUB, tq) * alpha[None] + pv[:HEAD_DIM].reshape(groups, SUB, tq)
        acc_ref[rows, :] = acc.reshape(HEAD_DIM, tq)


def _flash_init(m_ref, l_ref, acc_ref):
    m_ref[...] = jnp.full_like(m_ref, NEG)
    l_ref[...] = jnp.zeros_like(l_ref)
    acc_ref[...] = jnp.zeros_like(acc_ref)


def _flash_finish(o_ref, l_ref, acc_ref, tq):
    groups = HEAD_DIM // SUB
    for hp in range(o_ref.shape[1] // LANES):
        halves = []
        for h in (2 * hp, 2 * hp + 1):
            o = acc_ref[h * HEAD_DIM:(h + 1) * HEAD_DIM, :].reshape(groups, SUB, tq) / l_ref[h][None]
            halves.append(o.reshape(HEAD_DIM, tq))
        o_ref[:, hp * LANES:(hp + 1) * LANES] = jnp.concatenate(halves, axis=0).T.astype(o_ref.dtype)


def _dsa_index_kernel(qit_ref, wit_ref, ki_ref, out_ref, key_ref, *, tq, tk, topk):
    i = pl.program_id(1)
    n_chunks_total = out_ref.shape[0]
    nch = (i * tq + tq + tk - 1) // tk
    q_pos = i * tq + lax.broadcasted_iota(I32, (tk, tq), 1)
    k_row = lax.broadcasted_iota(I32, (tk, tq), 0)
    wi = wit_ref[...] * (IDX_HEADS ** -0.5)
    zeros_q = jnp.zeros((HEAD_DIM, tq), BF16)

    def score_chunk(c, carry):
        ks = ki_ref[pl.ds(pl.multiple_of(c * tk, tk), tk), :].astype(BF16)
        score = jnp.zeros((tk, tq), F32)
        for h in range(IDX_HEADS):
            qh = qit_ref[h * HEAD_DIM:(h + 1) * HEAD_DIM, :].astype(BF16)
            rel = jnp.maximum(_dot(ks, jnp.concatenate([qh, zeros_q], axis=0)), 0.0)
            score = score + wi[h:h + 1, :] * rel
        score = jnp.where(c * tk + k_row <= q_pos, score, -jnp.inf) + 0.0
        bits = lax.bitcast_convert_type(score, I32)
        key_ref[c] = jnp.where(bits < 0, bits ^ 0x7FFFFFFF, bits)
        return carry

    lax.fori_loop(0, nch, score_chunk, 0)

    def count(pred_fn):
        lanes = 4
        def body(c, accs):
            keys = key_ref[c]
            accs = list(accs)
            for g in range(tk // SUB):
                accs[g % lanes] = accs[g % lanes] + jnp.where(pred_fn(keys[g * SUB:(g + 1) * SUB, :]), 1.0, 0.0)
            return tuple(accs)
        accs = lax.fori_loop(0, nch, body, tuple(jnp.zeros((SUB, tq), F32) for _ in range(lanes)))
        acc = (accs[0] + accs[1]) + (accs[2] + accs[3])
        return jnp.sum(acc, axis=0, keepdims=True)

    def bit_step(b, t_u):
        cand_u = t_u | lax.shift_left(jnp.int32(1), 31 - b)
        cand = jnp.broadcast_to(cand_u ^ INT_MIN, (SUB, tq))
        cnt = count(lambda kk: kk >= cand)
        return jnp.where(cnt >= topk, cand_u, t_u)

    t_u = lax.fori_loop(0, 32, bit_step, jnp.zeros((1, tq), I32))
    thr = t_u ^ INT_MIN
    thr_s = jnp.broadcast_to(thr, (SUB, tq))
    need = topk - count(lambda kk: kk > thr_s)
    lt = jnp.where(lax.broadcasted_iota(I32, (tk, tk), 1) < lax.broadcasted_iota(I32, (tk, tk), 0), 1.0, 0.0).astype(BF16)

    def mask_chunk(c, seen):
        keys = key_ref[c]
        eq = jnp.where(keys == thr, 1.0, 0.0)
        rank = _dot(lt, eq.astype(BF16)) + seen
        take = jnp.where(keys > thr, 1.0, jnp.where(rank < need, eq, 0.0))
        take = jnp.where(c * tk + k_row <= q_pos, take, 0.0)
        out_ref[c] = jnp.where(take > 0.5, 0.0, NEG).astype(out_ref.dtype)
        return seen + jnp.sum(eq, axis=0, keepdims=True)

    lax.fori_loop(0, nch, mask_chunk, jnp.zeros((1, tq), F32))

    def fill_chunk(c, carry):
        out_ref[c] = jnp.full((tk, tq), NEG, out_ref.dtype)
        return carry

    lax.fori_loop(nch, n_chunks_total, fill_chunk, 0)


def _dsa_index(qqt, wit, kk, b, s, d, tq=256, tk=512):
    nq, nk = s // tq, s // tk
    n_qi = IDX_HEADS * HEAD_DIM
    return pl.pallas_call(
        functools.partial(_dsa_index_kernel, tq=tq, tk=tk, topk=min(TOPK_TOKENS, s // 4)),
        grid=(b, nq),
        in_specs=[
            pl.BlockSpec((n_qi, tq), lambda bb, i: (d // n_qi, bb * nq + i)),
            pl.BlockSpec((IDX_HEADS, tq), lambda bb, i: (0, bb * nq + i)),
            pl.BlockSpec((s, LANES), lambda bb, i: (bb, d // LANES)),
        ],
        out_specs=pl.BlockSpec((None, None, nk, tk, tq), lambda bb, i: (bb, i, 0, 0, 0)),
        out_shape=jax.ShapeDtypeStruct((b, nq, nk, tk, tq), BF16),
        scratch_shapes=[pltpu.VMEM((nk, tk, tq), I32)],
        compiler_params=_cparams("parallel", "arbitrary"),
        name="dsa_index",
    )(qqt, wit, kk)


def _dsa_attn_kernel(qi_ref, kt_ref, first_ref, last_ref, qt_ref, k_ref, vt_ref, bias_ref, o_ref,
                     m_ref, l_ref, acc_ref, s_ref, *, tq, tk):
    s = pl.program_id(1)

    @pl.when(first_ref[s] == 1)
    def _():
        _flash_init(m_ref, l_ref, acc_ref)

    bias = bias_ref[...].astype(F32)
    _flash_sweep(qt_ref, k_ref, vt_ref, m_ref, l_ref, acc_ref, s_ref, lambda h: bias, tk, tq)

    @pl.when(last_ref[s] == 1)
    def _():
        _flash_finish(o_ref, l_ref, acc_ref, tq)


def _flash_specs(b, s, d, tq, tk, tabs, bias_spec, slots, vt_row_block=0):
    n_heads = d // HEAD_DIM
    nq, nk = s // tq, s // tk
    return pltpu.PrefetchScalarGridSpec(
        num_scalar_prefetch=4,
        grid=(b, int(tabs[0].shape[0])),
        in_specs=[
            pl.BlockSpec((d, tq), lambda bb, ss, qi, kt, fi, la: (0, bb * nq + qi[ss])),
            pl.BlockSpec((tk, d), lambda bb, ss, qi, kt, fi, la: (bb * nk + kt[ss], 0)),
            pl.BlockSpec((d, tk), lambda bb, ss, qi, kt, fi, la: (vt_row_block, bb * nk + kt[ss])),
            bias_spec,
        ],
        out_specs=pl.BlockSpec((tq, d), lambda bb, ss, qi, kt, fi, la: (bb * nq + qi[ss], 0)),
        scratch_shapes=[pltpu.VMEM((n_heads, SUB, tq), F32), pltpu.VMEM((n_heads, SUB, tq), F32),
                        pltpu.VMEM((d, tq), F32), pltpu.VMEM((slots, tk, tq), F32)],
    )


def _dsa_attention(qqt, kk, vt, bias, b, s, d, tq=256, tk=512):
    tabs = _pair_tables(s // tq, lambda i: (i * tq + tq - 1) // tk, descending=False)
    bias_spec = pl.BlockSpec((None, None, None, tk, tq), lambda bb, ss, qi, kt, fi, la: (bb, qi[ss], kt[ss], 0, 0))
    return pl.pallas_call(
        functools.partial(_dsa_attn_kernel, tq=tq, tk=tk),
        grid_spec=_flash_specs(b, s, d, tq, tk, tabs, bias_spec, slots=4),
        out_shape=jax.ShapeDtypeStruct((b * s, d), BF16),
        compiler_params=_cparams("parallel", "arbitrary"),
        name="dsa_attn",
    )(*tabs, qqt, kk, vt, bias)


def _block_mean_kernel(k_ref, o_ref):
    rows = k_ref.shape[0]
    d = k_ref.shape[1]
    o_ref[...] = jnp.sum(k_ref[...].reshape(rows // MOBA_BLOCK, MOBA_BLOCK, d), axis=1) * (1.0 / MOBA_BLOCK)


def _block_means(k, b, s, d, rows=2048):
    per = rows // MOBA_BLOCK
    per_b = s // rows
    return pl.pallas_call(
        _block_mean_kernel,
        grid=(b, per_b),
        in_specs=[pl.BlockSpec((rows, d), lambda bb, i: (bb * per_b + i, 0))],
        out_specs=pl.BlockSpec((None, per, d), lambda bb, i: (bb, i, 0)),
        out_shape=jax.ShapeDtypeStruct((b, s // MOBA_BLOCK, d), F32),
        compiler_params=_cparams("parallel", "parallel"),
        name="moba_block_mean",
    )(k)


def _moba_select_kernel(qt_ref, kmt_ref, o_ref, *, nb, n_heads, topk):
    cur = pl.program_id(1)
    tq = qt_ref.shape[1]
    width = nb * n_heads
    gate = _dot_precise(kmt_ref[...], qt_ref[...])
    blk = lax.broadcasted_iota(I32, (width, tq), 0) // n_heads
    gate = jnp.where(blk < cur, gate, -jnp.inf)
    rank = jnp.zeros((width, tq), F32)
    for r in range(1, nb):
        sh = r * n_heads
        other = jnp.concatenate([gate[width - sh:], gate[:width - sh]], axis=0)
        tie_first = jnp.where(blk >= r, 1.0, 0.0)
        rank = rank + jnp.where(other > gate, 1.0, jnp.where(other == gate, tie_first, 0.0))
    bias = jnp.where(blk < cur, jnp.where(rank < topk, 0.0, NEG), NEG)
    o_ref[...] = bias.reshape(nb, n_heads, tq)


def _moba_select(qt, kmt, b, s, d):
    nb = s // MOBA_BLOCK
    n_heads = d // HEAD_DIM
    topk = min(MOBA_TOPK, nb - 1)
    return pl.pallas_call(
        functools.partial(_moba_select_kernel, nb=nb, n_heads=n_heads, topk=topk),
        grid=(b, nb),
        in_specs=[
            pl.BlockSpec((d, MOBA_BLOCK), lambda bb, i: (0, bb * nb + i)),
            pl.BlockSpec((None, nb * n_heads, d), lambda bb, i: (bb, 0, 0)),
        ],
        out_specs=pl.BlockSpec((None, nb, n_heads, MOBA_BLOCK), lambda bb, i: (bb, 0, 0, i)),
        out_shape=jax.ShapeDtypeStruct((b, nb, n_heads, s), F32),
        compiler_params=_cparams("parallel", "parallel"),
        name="moba_select",
    )(qt, kmt)


def _moba_attn_kernel(qi_ref, kt_ref, first_ref, last_ref, qt_ref, k_ref, vt_ref, sel_ref, o_ref,
                      m_ref, l_ref, acc_ref, s_ref, *, tq):
    s = pl.program_id(1)

    @pl.when(first_ref[s] == 1)
    def _():
        _flash_init(m_ref, l_ref, acc_ref)

    @pl.when(last_ref[s] == 0)
    def _():
        sel = sel_ref[...]
        _flash_sweep(qt_ref, k_ref, vt_ref, m_ref, l_ref, acc_ref, s_ref, lambda h: sel[h:h + 1, :], tq, tq)

    @pl.when(last_ref[s] == 1)
    def _():
        k_row = lax.broadcasted_iota(I32, (tq, tq), 0)
        q_col = lax.broadcasted_iota(I32, (tq, tq), 1)
        causal = jnp.where(k_row <= q_col, 0.0, NEG)
        _flash_sweep(qt_ref, k_ref, vt_ref, m_ref, l_ref, acc_ref, s_ref, lambda h: causal, tq, tq)
        _flash_finish(o_ref, l_ref, acc_ref, tq)


def _moba_attention(qt, k, vt, sel, b, s, d):
    tq = MOBA_BLOCK
    tabs = _pair_tables(s // tq, lambda i: i, descending=False)
    n_heads = d // HEAD_DIM
    bias_spec = pl.BlockSpec((None, None, n_heads, tq), lambda bb, ss, qi, kt, fi, la: (bb, kt[ss], 0, qi[ss]))
    return pl.pallas_call(
        functools.partial(_moba_attn_kernel, tq=tq),
        grid_spec=_flash_specs(b, s, d, tq, tq, tabs, bias_spec, slots=6),
        out_shape=jax.ShapeDtypeStruct((b * s, d), BF16),
        compiler_params=_cparams("parallel", "arbitrary"),
        name="moba_attn",
    )(*tabs, qt, k, vt, sel)


def _rope_lane_tables(positions):
    half = ROT_DIM // 2
    inv_freq = ROPE_THETA ** (-jnp.arange(0, ROT_DIM, 2, dtype=F32) / ROT_DIM)
    ang = positions.astype(F32).reshape(-1, 1) * inv_freq
    cos, sin = jnp.cos(ang), jnp.sin(ang)
    n = cos.shape[0]
    pad = HEAD_DIM - ROT_DIM
    c = jnp.concatenate([cos, cos, jnp.ones((n, pad), F32)], axis=1)
    sa = jnp.concatenate([-sin, jnp.zeros((n, half + pad), F32)], axis=1)
    sb = jnp.concatenate([jnp.zeros((n, half), F32), sin, jnp.zeros((n, pad), F32)], axis=1)
    rep = LANES // HEAD_DIM
    return tuple(jnp.tile(t, (1, rep)) for t in (c, sa, sb))


def _moba_gate_matrix(kmean, n_heads):
    b, nb, d = kmean.shape
    head_of_col = jnp.arange(d) // HEAD_DIM
    onehot = (jnp.arange(n_heads)[:, None] == head_of_col[None, :]).astype(F32)
    return (kmean[:, :, None, :] * onehot[None, None, :, :]).reshape(b, nb * n_heads, d)


def kernel(x, p, positions, w_in_sb, w_out_sb, w_in_dil, w_out_dil, w_in_dsa, w_out_dsa, w_in_moba, w_out_moba,
           g_mix_pre, g_mix_post, g_ffn_pre, g_ffn_post, w_ff_in, w_ff_out, g_ple, w_ple_gate, w_ple):
    b, s, d = x.shape
    depth = p.shape[0]
    n = b * s
    n_heads = d // HEAD_DIM
    rope = _rope_lane_tables(positions)
    h = x.reshape(n, d)
    for i in range(depth):
        mixer, j = i % 4, i // 4
        if mixer == 0:
            qkv = _project(h, g_mix_pre[i], w_in_sb[j].astype(BF16), BF16)
            o = _sb_attention(qkv.reshape(b, s, 3 * d), b, s, d).reshape(n, d)
            h = _outproj(o, w_out_sb[j].astype(BF16), g_mix_post[i], h)
        elif mixer == 1:
            gw = w_out_dil.shape[1]
            w = w_in_dil[j].astype(BF16)
            outs, lses, dils = [], [], [dil for _, dil in DIL_CONFIGS]
            for g, dil in enumerate(dils):
                proj = _project(h, g_mix_pre[i], w[:, 3 * gw * g:3 * gw * (g + 1)], BF16, rope, rope_cols=2 * gw,
                                tm=512, tn=3 * gw, layout="dilated", dil=dil)
                o_g, lse_g = _banded_attention(proj, b, s, dil, gw)
                outs.append(o_g)
                lses.append(lse_g)
            h = _dil_outproj(outs, lses, dils, w_out_dil[j].astype(BF16), g_mix_post[i], h)
        elif mixer == 2:
            w = w_in_dsa[j].astype(BF16)
            n_qi = IDX_HEADS * HEAD_DIM
            w_q, w_k, w_v = w[:, :d], w[:, d:2 * d], w[:, 2 * d:3 * d]
            w_qi = w[:, 3 * d:3 * d + n_qi]
            w_ki = w[:, 3 * d + n_qi:3 * d + n_qi + HEAD_DIM]
            w_wi = jnp.pad(w[:, 3 * d + n_qi + HEAD_DIM:], ((0, 0), (0, LANES - IDX_HEADS)))
            gp = g_mix_pre[i]
            qqt = _project(h, gp, jnp.concatenate([w_q, w_qi], axis=1), BF16, rope, rope_cols=d + n_qi, layout="cols")
            kk = _project(h, gp, jnp.concatenate([w_k, w_ki, w_ki], axis=1), BF16, rope, rope_cols=d + LANES, tn=3 * LANES)
            vt = _project(h, gp, w_v, BF16, layout="cols")
            wit = _project(h, gp, w_wi, F32, tn=LANES, layout="cols")
            bias = _dsa_index(qqt, wit, kk, b, s, d)
            o = _dsa_attention(qqt, kk, vt, bias, b, s, d)
            h = _outproj(o, w_out_dsa[j].astype(BF16), g_mix_post[i], h)
        else:
            w = w_in_moba[j].astype(BF16)
            gp = g_mix_pre[i]
            qt = _project(h, gp, w[:, :d], F32, rope, rope_cols=d, layout="cols")
            k = _project(h, gp, w[:, d:2 * d], F32, rope, rope_cols=d)
            vt = _project(h, gp, w[:, 2 * d:], BF16, layout="cols")
            kmt = _moba_gate_matrix(_block_means(k, b, s, d), n_heads)
            sel = _moba_select(qt, kmt, b, s, d)
            o = _moba_attention(qt, k, vt, sel, b, s, d)
            h = _outproj(o, w_out_moba[j].astype(BF16), g_mix_post[i], h)
        h = _ffn_ple(h, g_ffn_pre[i], w_ff_in[i].astype(BF16), w_ff_out[i].astype(BF16), g_ffn_post[i],
                     g_ple[i], w_ple_gate[i].astype(BF16), p[i].reshape(n, -1), w_ple[i].astype(BF16))
    return h.reshape(b, s, d)
```

```python
import functools

import jax
import jax.numpy as jnp
import numpy as np
from jax import lax
from jax.experimental import pallas as pl
from jax.experimental.pallas import tpu as pltpu

F32 = jnp.float32
BF16 = jnp.bfloat16
I32 = jnp.int32

LANES = 128
SUB = 8
SUB_BF16 = 16
HEAD_DIM = 64
HALF = HEAD_DIM
ROT_DIM = HEAD_DIM // 4
ROPE_THETA = 500000.0
EPS = 1e-6
NEG = -1e30
SCALE = HEAD_DIM ** -0.5
QK_SCALE_LOG2 = SCALE * 1.4426950408889634
DIL_CONFIGS = ((128, 1), (512, 4), (2048, 16))
BAND = 128
IDX_HEADS = 8
TOPK_TOKENS = 256
MOBA_BLOCK = 256
MOBA_TOPK = 3
INT_MIN = -2 ** 31
VMEM_LIMIT = 52 * 1024 * 1024


def _cparams(*sem):
    return pltpu.CompilerParams(dimension_semantics=sem, vmem_limit_bytes=VMEM_LIMIT)


def _rms(x, g):
    return x * lax.rsqrt(jnp.mean(x * x, axis=-1, keepdims=True) + EPS) * g


def _dot(a, b):
    return jnp.dot(a, b, preferred_element_type=F32)


def _dot_nt(a, b):
    return lax.dot_general(a, b, (((1,), (1,)), ((), ())), preferred_element_type=F32)


def _split3(a):
    hi = a.astype(BF16)
    lo = (a - hi.astype(F32)).astype(BF16)
    return hi, lo


def _dot_precise(a, b):
    a_hi, a_lo = _split3(a)
    b_hi, b_lo = _split3(b)
    return _dot(a_hi, b_hi) + _dot(a_hi, b_lo) + _dot(a_lo, b_hi)


def _lane_lt_half(shape):
    return lax.broadcasted_iota(I32, shape, 1) < HALF


def _proj_kernel(*refs, rope_cols, rope_all, tn, layout, dil, scale_cols, scale):
    refs = list(refs)
    y_ref = refs.pop() if layout == "dilated" else None
    if rope_cols:
        x_ref, g_ref, w_ref, c_ref, sa_ref, sb_ref, o_ref, xn_ref = refs
    else:
        x_ref, g_ref, w_ref, o_ref, xn_ref = refs
    j = pl.program_id(1)
    tm = x_ref.shape[0]

    @pl.when(j == 0)
    def _():
        xn_ref[...] = _rms(x_ref[...], g_ref[...]).astype(BF16)

    y = _dot(xn_ref[...], w_ref[...])
    if scale_cols:
        y = y * jnp.where(j < scale_cols // tn, scale, 1.0)

    def emit(rope_chunks):
        for u in range(tn // LANES):
            lanes = slice(u * LANES, (u + 1) * LANES)
            yu = y[:, lanes]
            if u < rope_chunks:
                yu = (yu * c_ref[...] + pltpu.roll(yu, LANES - ROT_DIM // 2, 1) * sa_ref[...]
                      + pltpu.roll(yu, ROT_DIM // 2, 1) * sb_ref[...])
            if layout == "rows":
                o_ref[:, lanes] = yu.astype(o_ref.dtype)
            elif layout == "cols":
                o_ref[lanes, :] = yu.T.astype(o_ref.dtype)
            else:
                y_ref[u] = yu
        if layout == "dilated":
            for r in range(dil):
                for u in range(tn // LANES):
                    col = r * tn + u * LANES
                    o_ref[:, col:col + LANES] = y_ref[u, pl.ds(r, tm // dil, stride=dil), :].astype(o_ref.dtype)

    if rope_cols == 0:
        emit(0)
    elif layout == "dilated":
        emit(min(rope_cols, tn) // LANES)
    elif rope_all:
        emit(tn // LANES)
    else:
        rope_blocks = rope_cols // tn

        @pl.when(j < rope_blocks)
        def _():
            emit(tn // LANES)

        @pl.when(j >= rope_blocks)
        def _():
            emit(0)


def _project(h, g, w, out_dtype, rope=None, rope_cols=0, tm=1024, tn=512, layout="rows", dil=1, scale_cols=0, scale=1.0):
    n, d = h.shape
    nout = w.shape[1]
    in_specs = [
        pl.BlockSpec((tm, d), lambda i, j: (i, 0)),
        pl.BlockSpec((1, d), lambda i, j: (0, 0)),
        pl.BlockSpec((d, tn), lambda i, j: (0, j)),
    ]
    args = [h, g.reshape(1, d), w]
    if rope_cols:
        assert rope_cols % LANES == 0 and (layout == "dilated" or rope_cols % tn == 0)
        in_specs += [pl.BlockSpec((tm, LANES), lambda i, j: (i, 0))] * 3
        args += list(rope)
    scratch = [pltpu.VMEM((tm, d), BF16)]
    if layout == "rows":
        out_spec = pl.BlockSpec((tm, tn), lambda i, j: (i, j))
        out_shape = (n, nout)
    elif layout == "cols":
        out_spec = pl.BlockSpec((tn, tm), lambda i, j: (j, i))
        out_shape = (nout, n)
    else:
        assert tn == nout and tm % (dil * SUB) == 0
        out_spec = pl.BlockSpec((tm // dil, dil * nout), lambda i, j: (i, 0))
        out_shape = (n // dil, dil * nout)
        scratch.append(pltpu.VMEM((tn // LANES, tm, LANES), F32))
    return pl.pallas_call(
        functools.partial(_proj_kernel, rope_cols=rope_cols, rope_all=rope_cols == nout, tn=tn, layout=layout, dil=dil,
                          scale_cols=scale_cols, scale=scale),
        grid=(n // tm, nout // tn),
        in_specs=in_specs,
        out_specs=out_spec,
        out_shape=jax.ShapeDtypeStruct(out_shape, out_dtype),
        scratch_shapes=scratch,
        compiler_params=_cparams("parallel", "arbitrary"),
        name="proj",
    )(*args)


def _outproj_kernel(o_ref, w_ref, g_ref, h_ref, out_ref):
    y = _dot(o_ref[...], w_ref[...])
    out_ref[...] = h_ref[...] + _rms(y, g_ref[...])


def _outproj(o, w, g, h, tm=512):
    n, d = h.shape
    k = o.shape[1]
    return pl.pallas_call(
        _outproj_kernel,
        grid=(n // tm,),
        in_specs=[
            pl.BlockSpec((tm, k), lambda i: (i, 0)),
            pl.BlockSpec((k, d), lambda i: (0, 0)),
            pl.BlockSpec((1, d), lambda i: (0, 0)),
            pl.BlockSpec((tm, d), lambda i: (i, 0)),
        ],
        out_specs=pl.BlockSpec((tm, d), lambda i: (i, 0)),
        out_shape=jax.ShapeDtypeStruct((n, d), F32),
        compiler_params=_cparams("parallel"),
        name="outproj",
    )(o, w, g.reshape(1, d), h)


def _dil_outproj_kernel(*refs, dils):
    n_g = len(dils)
    o_refs, l_refs = refs[:n_g], refs[n_g:2 * n_g]
    w_ref, g_ref, h_ref, out_ref = refs[2 * n_g:2 * n_g + 4]
    scr = refs[2 * n_g + 4:]
    tm = h_ref.shape[0]
    gw = w_ref.shape[0]

    def natural(ref, scr_ref, dil):
        if dil == 1:
            return ref[...]
        for r in range(dil):
            for u in range(gw // LANES):
                col = r * gw + u * LANES
                scr_ref[u, pl.ds(r, tm // dil, stride=dil), :] = ref[:, col:col + LANES]
        return jnp.concatenate([scr_ref[u] for u in range(gw // LANES)], axis=1)

    os_ = [natural(o_refs[i], scr[i], dils[i]) for i in range(n_g)]
    ls_ = [natural(l_refs[i], scr[n_g + i], dils[i]) for i in range(n_g)]
    m = functools.reduce(jnp.maximum, ls_)
    es = [jnp.exp(l - m) for l in ls_]
    o = sum(e * o for e, o in zip(es, os_)) / sum(es)
    y = _dot(o.astype(BF16), w_ref[...])
    out_ref[...] = h_ref[...] + _rms(y, g_ref[...])


def _dil_outproj(os_, ls_, dils, w, g, h, tm=512):
    n, d = h.shape
    k = w.shape[0]
    tiles = [pl.BlockSpec((tm // dil, dil * k), lambda i: (i, 0)) for dil in dils]
    return pl.pallas_call(
        functools.partial(_dil_outproj_kernel, dils=tuple(dils)),
        grid=(n // tm,),
        in_specs=tiles + tiles + [
            pl.BlockSpec((k, d), lambda i: (0, 0)),
            pl.BlockSpec((1, d), lambda i: (0, 0)),
            pl.BlockSpec((tm, d), lambda i: (i, 0)),
        ],
        out_specs=pl.BlockSpec((tm, d), lambda i: (i, 0)),
        out_shape=jax.ShapeDtypeStruct((n, d), F32),
        scratch_shapes=[pltpu.VMEM((k // LANES, tm, LANES), F32)] * (2 * len(dils)),
        compiler_params=_cparams("parallel"),
        name="dil_outproj",
    )(*os_, *ls_, w, g.reshape(1, d), h)


def _ffn_kernel(h_ref, gpre_ref, w1_ref, w2_ref, gpost_ref, gple_ref, wg_ref, p_ref, wp_ref,
                out_ref, un_ref, acc_ref, *, nf):
    f = pl.program_id(1)

    @pl.when(f == 0)
    def _():
        un_ref[...] = _rms(h_ref[...], gpre_ref[...]).astype(BF16)
        acc_ref[...] = jnp.zeros_like(acc_ref)

    a = _dot(un_ref[...], w1_ref[...])
    a = jnp.square(jnp.maximum(a, 0.0)).astype(BF16)
    acc_ref[...] += _dot(a, w2_ref[...])

    @pl.when(f == nf - 1)
    def _():
        h = h_ref[...] + _rms(acc_ref[...], gpost_ref[...])
        u = _rms(h, gple_ref[...]).astype(BF16)
        gate = 1.0 / (1.0 + jnp.exp(-_dot(u, wg_ref[...])))
        e = _dot(p_ref[...].astype(BF16), wp_ref[...])
        out_ref[...] = h + e * gate


def _ffn_ple(h, gpre, w1, w2, gpost, gple, wg, p, wp, tm=1024, tf=512):
    n, d = h.shape
    dff = w1.shape[1]
    pd = p.shape[1]
    nf = dff // tf
    row = lambda i, f: (i, 0)
    const = lambda i, f: (0, 0)
    return pl.pallas_call(
        functools.partial(_ffn_kernel, nf=nf),
        grid=(n // tm, nf),
        in_specs=[
            pl.BlockSpec((tm, d), row),
            pl.BlockSpec((1, d), const),
            pl.BlockSpec((d, tf), lambda i, f: (0, f)),
            pl.BlockSpec((tf, d), lambda i, f: (f, 0)),
            pl.BlockSpec((1, d), const),
            pl.BlockSpec((1, d), const),
            pl.BlockSpec((d, d), const),
            pl.BlockSpec((tm, pd), row),
            pl.BlockSpec((pd, d), const),
        ],
        out_specs=pl.BlockSpec((tm, d), row),
        out_shape=jax.ShapeDtypeStruct((n, d), F32),
        scratch_shapes=[pltpu.VMEM((tm, d), BF16), pltpu.VMEM((tm, d), F32)],
        compiler_params=_cparams("parallel", "arbitrary"),
        name="ffn_ple",
    )(h, gpre.reshape(1, d), w1, w2, gpost.reshape(1, d), gple.reshape(1, d), wg, p, wp)


def _pair_tables(nq, last_kt_of, descending):
    qi, kt, first, last = [], [], [], []
    for i in range(nq):
        kts = list(range(last_kt_of(i) + 1))
        if descending:
            kts = kts[::-1]
        for n, t in enumerate(kts):
            qi.append(i)
            kt.append(t)
            first.append(1 if n == 0 else 0)
            last.append(1 if n == len(kts) - 1 else 0)
    mk = lambda v: jnp.asarray(np.asarray(v, dtype=np.int32))
    return mk(qi), mk(kt), mk(first), mk(last)


def _sbt_kernel(qi_ref, kt_ref, first_ref, last_ref, qt_ref, k_ref, vt_ref, o_ref, acc_ref, carry_ref, z_ref, *, tq, tk):
    s = pl.program_id(1)
    n_heads = qt_ref.shape[0] // HEAD_DIM
    past = lax.broadcasted_iota(I32, (tk, tq), 0) < lax.broadcasted_iota(I32, (tk, tq), 1)
    upper = jnp.where(lax.broadcasted_iota(I32, (tk, tk), 1) > lax.broadcasted_iota(I32, (tk, tk), 0), 1.0, 0.0)
    w_aug = jnp.concatenate([upper.astype(BF16), jnp.ones((SUB_BF16, tk), BF16)], axis=0)
    zeros_q = jnp.zeros((HEAD_DIM, tq), BF16)

    def logits(h):
        qh = qt_ref[h * HEAD_DIM:(h + 1) * HEAD_DIM, :]
        qm = jnp.concatenate([qh, zeros_q] if h % 2 == 0 else [zeros_q, qh], axis=0)
        return _dot(k_ref[:, (h // 2) * LANES:(h // 2 + 1) * LANES], qm)

    def sweep(diag):
        def finish(h, zs, later):
            a = jnp.exp(zs - later)
            if diag:
                a = jnp.where(past, a, 0.0)
            rows = slice(h * HEAD_DIM, (h + 1) * HEAD_DIM)
            acc_ref[rows, :] += _dot(vt_ref[rows, :], a.astype(BF16))

        slots = z_ref.shape[0]
        for h in range(min(slots - 1, n_heads)):
            z_ref[h] = logits(h)
        pending = None
        for h in range(n_heads):
            z = z_ref[h % slots]
            if h + slots - 1 < n_heads:
                z_ref[(h + slots - 1) % slots] = logits(h + slots - 1)
            neg_abs = lax.bitcast_convert_type(lax.bitcast_convert_type(z, I32) | INT_MIN, F32)
            sp = jnp.maximum(z, 0.0) + jnp.log(1.0 + jnp.exp(neg_abs))
            if diag:
                sp = jnp.where(past, sp, 0.0)
            res = _dot(w_aug, sp.astype(BF16))
            c = carry_ref[h]
            later = (res[:tk].reshape(tk // SUB, SUB, tq) + c[None]).reshape(tk, tq)
            carry_ref[h] = c + res[tk:tk + SUB]
            if pending is not None:
                finish(*pending)
            pending = (h, z - sp, later)
        finish(*pending)

    @pl.when(first_ref[s] == 1)
    def _():
        acc_ref[...] = jnp.zeros_like(acc_ref)
        carry_ref[...] = jnp.zeros_like(carry_ref)
        sweep(True)

    @pl.when(first_ref[s] == 0)
    def _():
        sweep(False)

    @pl.when(last_ref[s] == 1)
    def _():
        for hp in range(o_ref.shape[1] // LANES):
            o_ref[:, hp * LANES:(hp + 1) * LANES] = acc_ref[hp * LANES:(hp + 1) * LANES, :].T.astype(o_ref.dtype)


def _sbt_attention(qt, k, vt, b, s, d, tq=256):
    nq = s // tq
    tabs = _pair_tables(nq, lambda i: i, descending=True)
    n_heads = d // HEAD_DIM
    grid_spec = pltpu.PrefetchScalarGridSpec(
        num_scalar_prefetch=4,
        grid=(b, int(tabs[0].shape[0])),
        in_specs=[
            pl.BlockSpec((d, tq), lambda bb, ss, qi, kt, fi, la: (0, bb * nq + qi[ss])),
            pl.BlockSpec((tq, d), lambda bb, ss, qi, kt, fi, la: (bb * nq + kt[ss], 0)),
            pl.BlockSpec((d, tq), lambda bb, ss, qi, kt, fi, la: (0, bb * nq + kt[ss])),
        ],
        out_specs=pl.BlockSpec((tq, d), lambda bb, ss, qi, kt, fi, la: (bb * nq + qi[ss], 0)),
        scratch_shapes=[pltpu.VMEM((d, tq), F32), pltpu.VMEM((n_heads, SUB, tq), F32),
                        pltpu.VMEM((4, tq, tq), F32)],
    )
    return pl.pallas_call(
        functools.partial(_sbt_kernel, tq=tq, tk=tq),
        grid_spec=grid_spec,
        out_shape=jax.ShapeDtypeStruct((b * s, d), BF16),
        compiler_params=_cparams("parallel", "arbitrary"),
        name="sbt_attn",
    )(*tabs, qt, k, vt)


def _banded_kernel(q_ref, kp_ref, kc_ref, vp_ref, vc_ref, o_ref, lse_ref, s_ref):
    i = pl.program_id(2)
    tq = q_ref.shape[0]
    n_pairs = q_ref.shape[1] // LANES
    row = lax.broadcasted_iota(I32, (tq, 2 * tq), 0)
    col = lax.broadcasted_iota(I32, (tq, 2 * tq), 1)
    lo_col = jnp.where(i > 0, row, jnp.maximum(row, tq))
    bias = jnp.where(col >= lo_col, jnp.where(col <= row + tq, 0.0, NEG), NEG)
    lo_half = _lane_lt_half((tq, LANES))
    for hp in range(n_pairs):
        sl = slice(hp * LANES, (hp + 1) * LANES)
        qp = q_ref[:, sl] * SCALE
        k2 = jnp.concatenate([kp_ref[:, sl], kc_ref[:, sl]], axis=0)
        for half in range(2):
            qm = jnp.where(lo_half if half == 0 else jnp.logical_not(lo_half), qp, jnp.zeros_like(qp))
            s_ref[2 * hp + half] = _dot_nt(qm, k2)
    for hp in range(n_pairs):
        sl = slice(hp * LANES, (hp + 1) * LANES)
        v2 = jnp.concatenate([vp_ref[:, sl], vc_ref[:, sl]], axis=0)
        outs, lses = [], []
        for half in range(2):
            sc = s_ref[2 * hp + half] + bias
            m = jnp.max(sc, axis=1, keepdims=True)
            p = jnp.exp(sc - m)
            l = jnp.sum(p, axis=1, keepdims=True)
            outs.append(_dot(p.astype(BF16), v2) / l)
            lses.append(jnp.broadcast_to(m + jnp.log(l), (tq, LANES)))
        o_ref[:, sl] = jnp.where(lo_half, outs[0], outs[1])
        lse_ref[:, sl] = jnp.where(lo_half, lses[0], lses[1])


def _banded_attention(proj, b, s, dil, gw):
    ncol = 3
    l = s // dil
    view = proj.reshape(b, l, dil * ncol * gw)
    nblk = l // BAND
    q_map = lambda bb, r, i: (bb, i, r * ncol)
    kc_map = lambda bb, r, i: (bb, i, r * ncol + 1)
    kp_map = lambda bb, r, i: (bb, jnp.maximum(i - 1, 0), r * ncol + 1)
    vc_map = lambda bb, r, i: (bb, i, r * ncol + 2)
    vp_map = lambda bb, r, i: (bb, jnp.maximum(i - 1, 0), r * ncol + 2)
    blk = lambda m: pl.BlockSpec((None, BAND, gw), m)
    out_map = lambda bb, r, i: (bb, i, r)
    o, lse = pl.pallas_call(
        _banded_kernel,
        grid=(b, dil, nblk),
        in_specs=[blk(q_map), blk(kp_map), blk(kc_map), blk(vp_map), blk(vc_map)],
        out_specs=[blk(out_map), blk(out_map)],
        out_shape=[jax.ShapeDtypeStruct((b, l, dil * gw), F32)] * 2,
        scratch_shapes=[pltpu.VMEM((gw // HEAD_DIM, BAND, 2 * BAND), F32)],
        compiler_params=_cparams("parallel", "parallel", "arbitrary"),
        name="banded_attn",
    )(view, view, view, view, view)
    return o.reshape(b * l, dil * gw), lse.reshape(b * l, dil * gw)


def _flash_sweep(qt_ref, k_ref, vt_ref, m_ref, l_ref, acc_ref, s_ref, bias_fn, tk, tq, q_scale):
    n_heads = qt_ref.shape[0] // HEAD_DIM
    zeros_q = jnp.zeros((HEAD_DIM, tq), BF16)
    ones_v = jnp.ones((SUB_BF16, tk), BF16)
    groups = HEAD_DIM // SUB

    def logits(h):
        qh = qt_ref[h * HEAD_DIM:(h + 1) * HEAD_DIM, :]
        if q_scale is not None:
            qh = qh * q_scale
        qh = qh.astype(BF16)
        qm = jnp.concatenate([qh, zeros_q] if h % 2 == 0 else [zeros_q, qh], axis=0)
        return _dot(k_ref[:, (h // 2) * LANES:(h // 2 + 1) * LANES].astype(BF16), qm)

    slots = s_ref.shape[0]
    for h in range(min(slots - 1, n_heads)):
        s_ref[h] = logits(h)
    for h in range(n_heads):
        sc = (s_ref[h % slots] + bias_fn(h)).reshape(tk // SUB, SUB, tq)
        if h + slots - 1 < n_heads:
            s_ref[(h + slots - 1) % slots] = logits(h + slots - 1)
        rows = slice(h * HEAD_DIM, (h + 1) * HEAD_DIM)
        m_prev = m_ref[h]
        m_new = jnp.maximum(m_prev, jnp.max(jnp.max(sc, axis=0), axis=0, keepdims=True))
        p = jnp.exp2(sc - m_new[None]).reshape(tk, tq).astype(BF16)
        alpha = jnp.exp2(m_prev - m_new)
        va = jnp.concatenate([vt_ref[rows, :].astype(BF16), ones_v], axis=0)
        pv = _dot(va, p)
        l_ref[h] = alpha * l_ref[h] + pv[HEAD_DIM:HEAD_DIM + SUB]
        m_ref[h] = m_new
        acc = acc_ref[rows, :].reshape(groups, SUB, tq) * alpha[None] + pv[:HEAD_DIM].reshape(groups, SUB, tq)
        acc_ref[rows, :] = acc.reshape(HEAD_DIM, tq)


def _flash_init(m_ref, l_ref, acc_ref):
    m_ref[...] = jnp.full_like(m_ref, NEG)
    l_ref[...] = jnp.zeros_like(l_ref)
    acc_ref[...] = jnp.zeros_like(acc_ref)


def _flash_finish(o_ref, l_ref, acc_ref, tq):
    groups = HEAD_DIM // SUB
    for hp in range(o_ref.shape[1] // LANES):
        halves = []
        for h in (2 * hp, 2 * hp + 1):
            o = acc_ref[h * HEAD_DIM:(h + 1) * HEAD_DIM, :].reshape(groups, SUB, tq) / l_ref[h][None]
            halves.append(o.reshape(HEAD_DIM, tq))
        o_ref[:, hp * LANES:(hp + 1) * LANES] = jnp.concatenate(halves, axis=0).T.astype(o_ref.dtype)


def _dsa_index_kernel(qit_ref, wit_ref, ki_ref, out_ref, key_ref, *, tq, tk, topk):
    i = pl.program_id(1)
    n_chunks_total = out_ref.shape[0]
    nch = (i * tq + tq + tk - 1) // tk
    q_pos = i * tq + lax.broadcasted_iota(I32, (tk, tq), 1)
    k_row = lax.broadcasted_iota(I32, (tk, tq), 0)
    wi = wit_ref[...] * (IDX_HEADS ** -0.5)
    zeros_q = jnp.zeros((HEAD_DIM, tq), BF16)

    def score_chunk(c, carry):
        ks = ki_ref[pl.ds(pl.multiple_of(c * tk, tk), tk), :].astype(BF16)
        score = jnp.zeros((tk, tq), F32)
        for h in range(IDX_HEADS):
            qh = qit_ref[h * HEAD_DIM:(h + 1) * HEAD_DIM, :].astype(BF16)
            rel = jnp.maximum(_dot(ks, jnp.concatenate([qh, zeros_q], axis=0)), 0.0)
            score = score + wi[h:h + 1, :] * rel
        score = jnp.where(c * tk + k_row <= q_pos, score, -jnp.inf) + 0.0
        bits = lax.bitcast_convert_type(score, I32)
        key_ref[c] = jnp.where(bits < 0, bits ^ 0x7FFFFFFF, bits)
        return carry

    lax.fori_loop(0, nch, score_chunk, 0)

    def count(pred_fn):
        lanes = 4
        def body(c, accs):
            keys = key_ref[c]
            accs = list(accs)
            for g in range(tk // SUB):
                accs[g % lanes] = accs[g % lanes] + jnp.where(pred_fn(keys[g * SUB:(g + 1) * SUB, :]), 1.0, 0.0)
            return tuple(accs)
        accs = lax.fori_loop(0, nch, body, tuple(jnp.zeros((SUB, tq), F32) for _ in range(lanes)))
        acc = (accs[0] + accs[1]) + (accs[2] + accs[3])
        return jnp.sum(acc, axis=0, keepdims=True)

    def bit_step(b, t_u):
        cand_u = t_u | lax.shift_left(jnp.int32(1), 31 - b)
        cand = jnp.broadcast_to(cand_u ^ INT_MIN, (SUB, tq))
        cnt = count(lambda kk: kk >= cand)
        return jnp.where(cnt >= topk, cand_u, t_u)

    t_u = lax.fori_loop(0, 32, bit_step, jnp.zeros((1, tq), I32))
    thr = t_u ^ INT_MIN
    thr_s = jnp.broadcast_to(thr, (SUB, tq))
    need = topk - count(lambda kk: kk > thr_s)
    lt = jnp.where(lax.broadcasted_iota(I32, (tk, tk), 1) < lax.broadcasted_iota(I32, (tk, tk), 0), 1.0, 0.0).astype(BF16)

    def mask_chunk(c, seen):
        keys = key_ref[c]
        eq = jnp.where(keys == thr, 1.0, 0.0)
        rank = _dot(lt, eq.astype(BF16)) + seen
        take = jnp.where(keys > thr, 1.0, jnp.where(rank < need, eq, 0.0))
        take = jnp.where(c * tk + k_row <= q_pos, take, 0.0)
        out_ref[c] = jnp.where(take > 0.5, 0.0, NEG).astype(out_ref.dtype)
        return seen + jnp.sum(eq, axis=0, keepdims=True)

    lax.fori_loop(0, nch, mask_chunk, jnp.zeros((1, tq), F32))

    def fill_chunk(c, carry):
        out_ref[c] = jnp.full((tk, tq), NEG, out_ref.dtype)
        return carry

    lax.fori_loop(nch, n_chunks_total, fill_chunk, 0)


def _dsa_index(qqt, wit, kk, b, s, d, tq=256, tk=512):
    nq, nk = s // tq, s // tk
    n_qi = IDX_HEADS * HEAD_DIM
    return pl.pallas_call(
        functools.partial(_dsa_index_kernel, tq=tq, tk=tk, topk=min(TOPK_TOKENS, s // 4)),
        grid=(b, nq),
        in_specs=[
            pl.BlockSpec((n_qi, tq), lambda bb, i: (d // n_qi, bb * nq + i)),
            pl.BlockSpec((IDX_HEADS, tq), lambda bb, i: (0, bb * nq + i)),
            pl.BlockSpec((s, LANES), lambda bb, i: (bb, d // LANES)),
        ],
        out_specs=pl.BlockSpec((None, None, nk, tk, tq), lambda bb, i: (bb, i, 0, 0, 0)),
        out_shape=jax.ShapeDtypeStruct((b, nq, nk, tk, tq), BF16),
        scratch_shapes=[pltpu.VMEM((nk, tk, tq), I32)],
        compiler_params=_cparams("parallel", "arbitrary"),
        name="dsa_index",
    )(qqt, wit, kk)


def _dsa_attn_kernel(qi_ref, kt_ref, first_ref, last_ref, qt_ref, k_ref, vt_ref, bias_ref, o_ref,
                     m_ref, l_ref, acc_ref, s_ref, *, tq, tk):
    s = pl.program_id(1)

    @pl.when(first_ref[s] == 1)
    def _():
        _flash_init(m_ref, l_ref, acc_ref)

    bias = bias_ref[...].astype(F32)
    _flash_sweep(qt_ref, k_ref, vt_ref, m_ref, l_ref, acc_ref, s_ref, lambda h: bias, tk, tq, None)

    @pl.when(last_ref[s] == 1)
    def _():
        _flash_finish(o_ref, l_ref, acc_ref, tq)


def _flash_specs(b, s, d, tq, tk, tabs, bias_spec, slots, vt_row_block=0):
    n_heads = d // HEAD_DIM
    nq, nk = s // tq, s // tk
    return pltpu.PrefetchScalarGridSpec(
        num_scalar_prefetch=4,
        grid=(b, int(tabs[0].shape[0])),
        in_specs=[
            pl.BlockSpec((d, tq), lambda bb, ss, qi, kt, fi, la: (0, bb * nq + qi[ss])),
            pl.BlockSpec((tk, d), lambda bb, ss, qi, kt, fi, la: (bb * nk + kt[ss], 0)),
            pl.BlockSpec((d, tk), lambda bb, ss, qi, kt, fi, la: (vt_row_block, bb * nk + kt[ss])),
            bias_spec,
        ],
        out_specs=pl.BlockSpec((tq, d), lambda bb, ss, qi, kt, fi, la: (bb * nq + qi[ss], 0)),
        scratch_shapes=[pltpu.VMEM((n_heads, SUB, tq), F32), pltpu.VMEM((n_heads, SUB, tq), F32),
                        pltpu.VMEM((d, tq), F32), pltpu.VMEM((slots, tk, tq), F32)],
    )


def _dsa_attention(qqt, kk, vt, bias, b, s, d, tq=256, tk=512):
    tabs = _pair_tables(s // tq, lambda i: (i * tq + tq - 1) // tk, descending=False)
    bias_spec = pl.BlockSpec((None, None, None, tk, tq), lambda bb, ss, qi, kt, fi, la: (bb, qi[ss], kt[ss], 0, 0))
    return pl.pallas_call(
        functools.partial(_dsa_attn_kernel, tq=tq, tk=tk),
        grid_spec=_flash_specs(b, s, d, tq, tk, tabs, bias_spec, slots=4),
        out_shape=jax.ShapeDtypeStruct((b * s, d), BF16),
        compiler_params=_cparams("parallel", "arbitrary"),
        name="dsa_attn",
    )(*tabs, qqt, kk, vt, bias)


def _block_mean_kernel(k_ref, o_ref):
    rows = k_ref.shape[0]
    d = k_ref.shape[1]
    o_ref[...] = jnp.sum(k_ref[...].reshape(rows // MOBA_BLOCK, MOBA_BLOCK, d), axis=1) * (1.0 / MOBA_BLOCK)


def _block_means(k, b, s, d, rows=2048):
    per = rows // MOBA_BLOCK
    per_b = s // rows
    return pl.pallas_call(
        _block_mean_kernel,
        grid=(b, per_b),
        in_specs=[pl.BlockSpec((rows, d), lambda bb, i: (bb * per_b + i, 0))],
        out_specs=pl.BlockSpec((None, per, d), lambda bb, i: (bb, i, 0)),
        out_shape=jax.ShapeDtypeStruct((b, s // MOBA_BLOCK, d), F32),
        compiler_params=_cparams("parallel", "parallel"),
        name="moba_block_mean",
    )(k)


def _moba_select_kernel(qt_ref, kmt_ref, o_ref, *, nb, n_heads, topk):
    cur = pl.program_id(1)
    tq = qt_ref.shape[1]
    width = nb * n_heads
    gate = _dot_precise(kmt_ref[...], qt_ref[...])
    blk = lax.broadcasted_iota(I32, (width, tq), 0) // n_heads
    gate = jnp.where(blk < cur, gate, -jnp.inf)
    rank = jnp.zeros((width, tq), F32)
    for r in range(1, nb):
        sh = r * n_heads
        other = jnp.concatenate([gate[width - sh:], gate[:width - sh]], axis=0)
        tie_first = jnp.where(blk >= r, 1.0, 0.0)
        rank = rank + jnp.where(other > gate, 1.0, jnp.where(other == gate, tie_first, 0.0))
    bias = jnp.where(blk < cur, jnp.where(rank < topk, 0.0, NEG), NEG)
    o_ref[...] = bias.reshape(nb, n_heads, tq)


def _moba_select(qt, kmt, b, s, d):
    nb = s // MOBA_BLOCK
    n_heads = d // HEAD_DIM
    topk = min(MOBA_TOPK, nb - 1)
    return pl.pallas_call(
        functools.partial(_moba_select_kernel, nb=nb, n_heads=n_heads, topk=topk),
        grid=(b, nb),
        in_specs=[
            pl.BlockSpec((d, MOBA_BLOCK), lambda bb, i: (0, bb * nb + i)),
            pl.BlockSpec((None, nb * n_heads, d), lambda bb, i: (bb, 0, 0)),
        ],
        out_specs=pl.BlockSpec((None, nb, n_heads, MOBA_BLOCK), lambda bb, i: (bb, 0, 0, i)),
        out_shape=jax.ShapeDtypeStruct((b, nb, n_heads, s), F32),
        compiler_params=_cparams("parallel", "parallel"),
        name="moba_select",
    )(qt, kmt)


def _moba_attn_kernel(qi_ref, kt_ref, first_ref, last_ref, qt_ref, k_ref, vt_ref, sel_ref, o_ref,
                      m_ref, l_ref, acc_ref, s_ref, *, tq):
    s = pl.program_id(1)

    @pl.when(first_ref[s] == 1)
    def _():
        _flash_init(m_ref, l_ref, acc_ref)

    @pl.when(last_ref[s] == 0)
    def _():
        sel = sel_ref[...]
        _flash_sweep(qt_ref, k_ref, vt_ref, m_ref, l_ref, acc_ref, s_ref, lambda h: sel[h:h + 1, :], tq, tq, QK_SCALE_LOG2)

    @pl.when(last_ref[s] == 1)
    def _():
        k_row = lax.broadcasted_iota(I32, (tq, tq), 0)
        q_col = lax.broadcasted_iota(I32, (tq, tq), 1)
        causal = jnp.where(k_row <= q_col, 0.0, NEG)
        _flash_sweep(qt_ref, k_ref, vt_ref, m_ref, l_ref, acc_ref, s_ref, lambda h: causal, tq, tq, QK_SCALE_LOG2)
        _flash_finish(o_ref, l_ref, acc_ref, tq)


def _moba_attention(qt, k, vt, sel, b, s, d):
    tq = MOBA_BLOCK
    tabs = _pair_tables(s // tq, lambda i: i, descending=False)
    n_heads = d // HEAD_DIM
    bias_spec = pl.BlockSpec((None, None, n_heads, tq), lambda bb, ss, qi, kt, fi, la: (bb, kt[ss], 0, qi[ss]))
    return pl.pallas_call(
        functools.partial(_moba_attn_kernel, tq=tq),
        grid_spec=_flash_specs(b, s, d, tq, tq, tabs, bias_spec, slots=6),
        out_shape=jax.ShapeDtypeStruct((b * s, d), BF16),
        compiler_params=_cparams("parallel", "arbitrary"),
        name="moba_attn",
    )(*tabs, qt, k, vt, sel)


def _rope_lane_tables(positions):
    half = ROT_DIM // 2
    inv_freq = ROPE_THETA ** (-jnp.arange(0, ROT_DIM, 2, dtype=F32) / ROT_DIM)
    ang = positions.astype(F32).reshape(-1, 1) * inv_freq
    cos, sin = jnp.cos(ang), jnp.sin(ang)
    n = cos.shape[0]
    pad = HEAD_DIM - ROT_DIM
    c = jnp.concatenate([cos, cos, jnp.ones((n, pad), F32)], axis=1)
    sa = jnp.concatenate([-sin, jnp.zeros((n, half + pad), F32)], axis=1)
    sb = jnp.concatenate([jnp.zeros((n, half), F32), sin, jnp.zeros((n, pad), F32)], axis=1)
    rep = LANES // HEAD_DIM
    return tuple(jnp.tile(t, (1, rep)) for t in (c, sa, sb))


def _moba_gate_matrix(kmean, n_heads):
    b, nb, d = kmean.shape
    head_of_col = jnp.arange(d) // HEAD_DIM
    onehot = (jnp.arange(n_heads)[:, None] == head_of_col[None, :]).astype(F32)
    return (kmean[:, :, None, :] * onehot[None, None, :, :]).reshape(b, nb * n_heads, d)


def kernel(x, p, positions, w_in_sb, w_out_sb, w_in_dil, w_out_dil, w_in_dsa, w_out_dsa, w_in_moba, w_out_moba,
           g_mix_pre, g_mix_post, g_ffn_pre, g_ffn_post, w_ff_in, w_ff_out, g_ple, w_ple_gate, w_ple):
    b, s, d = x.shape
    depth = p.shape[0]
    n = b * s
    n_heads = d // HEAD_DIM
    rope = _rope_lane_tables(positions)
    h = x.reshape(n, d)
    for i in range(depth):
        mixer, j = i % 4, i // 4
        if mixer == 0:
            w = w_in_sb[j].astype(BF16)
            gp = g_mix_pre[i]
            qt = _project(h, gp, w[:, :d], BF16, layout="cols", scale_cols=d, scale=SCALE)
            k = _project(h, gp, w[:, d:2 * d], BF16)
            vt = _project(h, gp, w[:, 2 * d:], BF16, layout="cols")
            o = _sbt_attention(qt, k, vt, b, s, d)
            h = _outproj(o, w_out_sb[j].astype(BF16), g_mix_post[i], h)
        elif mixer == 1:
            gw = w_out_dil.shape[1]
            w = w_in_dil[j].astype(BF16)
            outs, lses, dils = [], [], [dil for _, dil in DIL_CONFIGS]
            for g, dil in enumerate(dils):
                proj = _project(h, g_mix_pre[i], w[:, 3 * gw * g:3 * gw * (g + 1)], BF16, rope, rope_cols=2 * gw,
                                tm=512, tn=3 * gw, layout="dilated", dil=dil)
                o_g, lse_g = _banded_attention(proj, b, s, dil, gw)
                outs.append(o_g)
                lses.append(lse_g)
            h = _dil_outproj(outs, lses, dils, w_out_dil[j].astype(BF16), g_mix_post[i], h)
        elif mixer == 2:
            w = w_in_dsa[j].astype(BF16)
            n_qi = IDX_HEADS * HEAD_DIM
            w_q, w_k, w_v = w[:, :d], w[:, d:2 * d], w[:, 2 * d:3 * d]
            w_qi = w[:, 3 * d:3 * d + n_qi]
            w_ki = w[:, 3 * d + n_qi:3 * d + n_qi + HEAD_DIM]
            w_wi = jnp.pad(w[:, 3 * d + n_qi + HEAD_DIM:], ((0, 0), (0, LANES - IDX_HEADS)))
            gp = g_mix_pre[i]
            qqt = _project(h, gp, jnp.concatenate([w_q, w_qi], axis=1), BF16, rope, rope_cols=d + n_qi, layout="cols",
                           scale_cols=d, scale=QK_SCALE_LOG2)
            kk = _project(h, gp, jnp.concatenate([w_k, w_ki, w_ki], axis=1), BF16, rope, rope_cols=d + LANES, tn=3 * LANES)
            vt = _project(h, gp, w_v, BF16, layout="cols")
            wit = _project(h, gp, w_wi, F32, tn=LANES, layout="cols")
            bias = _dsa_index(qqt, wit, kk, b, s, d)
            o = _dsa_attention(qqt, kk, vt, bias, b, s, d)
            h = _outproj(o, w_out_dsa[j].astype(BF16), g_mix_post[i], h)
        else:
            w = w_in_moba[j].astype(BF16)
            gp = g_mix_pre[i]
            qt = _project(h, gp, w[:, :d], F32, rope, rope_cols=d, layout="cols")
            k = _project(h, gp, w[:, d:2 * d], F32, rope, rope_cols=d)
            vt = _project(h, gp, w[:, 2 * d:], BF16, layout="cols")
            kmt = _moba_gate_matrix(_block_means(k, b, s, d), n_heads)
            sel = _moba_select(qt, kmt, b, s, d)
            o = _moba_attention(qt, k, vt, sel, b, s, d)
            h = _outproj(o, w_out_moba[j].astype(BF16), g_mix_post[i], h)
        h = _ffn_ple(h, g_ffn_pre[i], w_ff_in[i].astype(BF16), w_ff_out[i].astype(BF16), g_ffn_post[i],
                     g_ple[i], w_ple_gate[i].astype(BF16), p[i].reshape(n, -1), w_ple[i].astype(BF16))
    return h.reshape(b, s, d)
```

```python
import functools

import jax
import jax.numpy as jnp
import numpy as np
from jax import lax
from jax.experimental import pallas as pl
from jax.experimental.pallas import tpu as pltpu

F32 = jnp.float32
BF16 = jnp.bfloat16
I32 = jnp.int32

LANES = 128
SUB = 8
SUB_BF16 = 16
HEAD_DIM = 64
HALF = HEAD_DIM
ROT_DIM = HEAD_DIM // 4
ROPE_THETA = 500000.0
EPS = 1e-6
NEG = -1e30
SCALE = HEAD_DIM ** -0.5
QK_SCALE_LOG2 = SCALE * 1.4426950408889634
DIL_CONFIGS = ((128, 1), (512, 4), (2048, 16))
BAND = 128
IDX_HEADS = 8
TOPK_TOKENS = 256
MOBA_BLOCK = 256
MOBA_TOPK = 3
INT_MIN = -2 ** 31
VMEM_LIMIT = 52 * 1024 * 1024


def _cparams(*sem):
    return pltpu.CompilerParams(dimension_semantics=sem, vmem_limit_bytes=VMEM_LIMIT)


def _rms(x, g):
    return x * lax.rsqrt(jnp.mean(x * x, axis=-1, keepdims=True) + EPS) * g


def _dot(a, b):
    return jnp.dot(a, b, preferred_element_type=F32)


def _dot_nt(a, b):
    return lax.dot_general(a, b, (((1,), (1,)), ((), ())), preferred_element_type=F32)


def _split3(a):
    hi = a.astype(BF16)
    lo = (a - hi.astype(F32)).astype(BF16)
    return hi, lo


def _dot_precise(a, b):
    a_hi, a_lo = _split3(a)
    b_hi, b_lo = _split3(b)
    return _dot(a_hi, b_hi) + _dot(a_hi, b_lo) + _dot(a_lo, b_hi)


def _lane_lt_half(shape):
    return lax.broadcasted_iota(I32, shape, 1) < HALF


def _proj_kernel(*refs, rope_cols, rope_all, tn, layout, dil, scale_cols, scale):
    refs = list(refs)
    y_ref = refs.pop() if layout == "dilated" else None
    if rope_cols:
        x_ref, g_ref, w_ref, c_ref, sa_ref, sb_ref, o_ref, xn_ref = refs
    else:
        x_ref, g_ref, w_ref, o_ref, xn_ref = refs
    j = pl.program_id(1)
    tm = x_ref.shape[0]

    @pl.when(j == 0)
    def _():
        xn_ref[...] = _rms(x_ref[...], g_ref[...]).astype(BF16)

    y = _dot(xn_ref[...], w_ref[...])
    if scale_cols:
        y = y * jnp.where(j < scale_cols // tn, scale, 1.0)

    def emit(rope_chunks):
        for u in range(tn // LANES):
            lanes = slice(u * LANES, (u + 1) * LANES)
            yu = y[:, lanes]
            if u < rope_chunks:
                yu = (yu * c_ref[...] + pltpu.roll(yu, LANES - ROT_DIM // 2, 1) * sa_ref[...]
                      + pltpu.roll(yu, ROT_DIM // 2, 1) * sb_ref[...])
            if layout == "rows":
                o_ref[:, lanes] = yu.astype(o_ref.dtype)
            elif layout == "cols":
                o_ref[lanes, :] = yu.T.astype(o_ref.dtype)
            else:
                y_ref[u] = yu
        if layout == "dilated":
            for r in range(dil):
                for u in range(tn // LANES):
                    col = r * tn + u * LANES
                    o_ref[:, col:col + LANES] = y_ref[u, pl.ds(r, tm // dil, stride=dil), :].astype(o_ref.dtype)

    if rope_cols == 0:
        emit(0)
    elif layout == "dilated":
        emit(min(rope_cols, tn) // LANES)
    elif rope_all:
        emit(tn // LANES)
    else:
        rope_blocks = rope_cols // tn

        @pl.when(j < rope_blocks)
        def _():
            emit(tn // LANES)

        @pl.when(j >= rope_blocks)
        def _():
            emit(0)


def _project(h, g, w, out_dtype, rope=None, rope_cols=0, tm=1024, tn=512, layout="rows", dil=1, scale_cols=0, scale=1.0,
             cols=None):
    n, d = h.shape
    col0, nout = cols if cols is not None else (0, w.shape[1])
    assert col0 % tn == 0 and nout % tn == 0
    blk0 = col0 // tn
    in_specs = [
        pl.BlockSpec((tm, d), lambda i, j: (i, 0)),
        pl.BlockSpec((1, d), lambda i, j: (0, 0)),
        pl.BlockSpec((d, tn), lambda i, j: (0, j + blk0)),
    ]
    args = [h, g.reshape(1, d), w]
    if rope_cols:
        assert rope_cols % LANES == 0 and (layout == "dilated" or rope_cols % tn == 0)
        in_specs += [pl.BlockSpec((tm, LANES), lambda i, j: (i, 0))] * 3
        args += list(rope)
    scratch = [pltpu.VMEM((tm, d), BF16)]
    if layout == "rows":
        out_spec = pl.BlockSpec((tm, tn), lambda i, j: (i, j))
        out_shape = (n, nout)
    elif layout == "cols":
        out_spec = pl.BlockSpec((tn, tm), lambda i, j: (j, i))
        out_shape = (nout, n)
    else:
        assert tn == nout and tm % (dil * SUB) == 0
        out_spec = pl.BlockSpec((tm // dil, dil * nout), lambda i, j: (i, 0))
        out_shape = (n // dil, dil * nout)
        scratch.append(pltpu.VMEM((tn // LANES, tm, LANES), F32))
    return pl.pallas_call(
        functools.partial(_proj_kernel, rope_cols=rope_cols, rope_all=rope_cols == nout, tn=tn, layout=layout, dil=dil,
                          scale_cols=scale_cols, scale=scale),
        grid=(n // tm, nout // tn),
        in_specs=in_specs,
        out_specs=out_spec,
        out_shape=jax.ShapeDtypeStruct(out_shape, out_dtype),
        scratch_shapes=scratch,
        compiler_params=_cparams("parallel", "arbitrary"),
        name="proj",
    )(*args)


def _outproj_kernel(o_ref, w_ref, g_ref, h_ref, out_ref):
    y = _dot(o_ref[...], w_ref[...])
    out_ref[...] = h_ref[...] + _rms(y, g_ref[...])


def _outproj(o, w, g, h, tm=512):
    n, d = h.shape
    k = o.shape[1]
    return pl.pallas_call(
        _outproj_kernel,
        grid=(n // tm,),
        in_specs=[
            pl.BlockSpec((tm, k), lambda i: (i, 0)),
            pl.BlockSpec((k, d), lambda i: (0, 0)),
            pl.BlockSpec((1, d), lambda i: (0, 0)),
            pl.BlockSpec((tm, d), lambda i: (i, 0)),
        ],
        out_specs=pl.BlockSpec((tm, d), lambda i: (i, 0)),
        out_shape=jax.ShapeDtypeStruct((n, d), F32),
        compiler_params=_cparams("parallel"),
        name="outproj",
    )(o, w, g.reshape(1, d), h)


def _dil_outproj_kernel(*refs, dils):
    n_g = len(dils)
    o_refs, l_refs = refs[:n_g], refs[n_g:2 * n_g]
    w_ref, g_ref, h_ref, out_ref = refs[2 * n_g:2 * n_g + 4]
    scr = refs[2 * n_g + 4:]
    tm = h_ref.shape[0]
    gw = w_ref.shape[0]

    def natural(ref, scr_ref, dil):
        if dil == 1:
            return ref[...]
        for r in range(dil):
            for u in range(gw // LANES):
                col = r * gw + u * LANES
                scr_ref[u, pl.ds(r, tm // dil, stride=dil), :] = ref[:, col:col + LANES]
        return jnp.concatenate([scr_ref[u] for u in range(gw // LANES)], axis=1)

    os_ = [natural(o_refs[i], scr[i], dils[i]) for i in range(n_g)]
    ls_ = [natural(l_refs[i], scr[n_g + i], dils[i]) for i in range(n_g)]
    m = functools.reduce(jnp.maximum, ls_)
    es = [jnp.exp(l - m) for l in ls_]
    o = sum(e * o for e, o in zip(es, os_)) / sum(es)
    y = _dot(o.astype(BF16), w_ref[...])
    out_ref[...] = h_ref[...] + _rms(y, g_ref[...])


def _dil_outproj(os_, ls_, dils, w, g, h, tm=512):
    n, d = h.shape
    k = w.shape[0]
    tiles = [pl.BlockSpec((tm // dil, dil * k), lambda i: (i, 0)) for dil in dils]
    return pl.pallas_call(
        functools.partial(_dil_outproj_kernel, dils=tuple(dils)),
        grid=(n // tm,),
        in_specs=tiles + tiles + [
            pl.BlockSpec((k, d), lambda i: (0, 0)),
            pl.BlockSpec((1, d), lambda i: (0, 0)),
            pl.BlockSpec((tm, d), lambda i: (i, 0)),
        ],
        out_specs=pl.BlockSpec((tm, d), lambda i: (i, 0)),
        out_shape=jax.ShapeDtypeStruct((n, d), F32),
        scratch_shapes=[pltpu.VMEM((k // LANES, tm, LANES), F32)] * (2 * len(dils)),
        compiler_params=_cparams("parallel"),
        name="dil_outproj",
    )(*os_, *ls_, w, g.reshape(1, d), h)


def _ffn_kernel(h_ref, gpre_ref, w1_ref, w2_ref, gpost_ref, gple_ref, wg_ref, p_ref, wp_ref,
                out_ref, un_ref, acc_ref, *, nf):
    f = pl.program_id(1)

    @pl.when(f == 0)
    def _():
        un_ref[...] = _rms(h_ref[...], gpre_ref[...]).astype(BF16)
        acc_ref[...] = jnp.zeros_like(acc_ref)

    a = _dot(un_ref[...], w1_ref[...])
    a = jnp.square(jnp.maximum(a, 0.0)).astype(BF16)
    acc_ref[...] += _dot(a, w2_ref[...])

    @pl.when(f == nf - 1)
    def _():
        h = h_ref[...] + _rms(acc_ref[...], gpost_ref[...])
        u = _rms(h, gple_ref[...]).astype(BF16)
        gate = 1.0 / (1.0 + jnp.exp(-_dot(u, wg_ref[...])))
        e = _dot(p_ref[...].astype(BF16), wp_ref[...])
        out_ref[...] = h + e * gate


def _ffn_ple(h, gpre, w1, w2, gpost, gple, wg, p, wp, tm=1024, tf=512):
    n, d = h.shape
    dff = w1.shape[1]
    pd = p.shape[1]
    nf = dff // tf
    row = lambda i, f: (i, 0)
    const = lambda i, f: (0, 0)
    return pl.pallas_call(
        functools.partial(_ffn_kernel, nf=nf),
        grid=(n // tm, nf),
        in_specs=[
            pl.BlockSpec((tm, d), row),
            pl.BlockSpec((1, d), const),
            pl.BlockSpec((d, tf), lambda i, f: (0, f)),
            pl.BlockSpec((tf, d), lambda i, f: (f, 0)),
            pl.BlockSpec((1, d), const),
            pl.BlockSpec((1, d), const),
            pl.BlockSpec((d, d), const),
            pl.BlockSpec((tm, pd), row),
            pl.BlockSpec((pd, d), const),
        ],
        out_specs=pl.BlockSpec((tm, d), row),
        out_shape=jax.ShapeDtypeStruct((n, d), F32),
        scratch_shapes=[pltpu.VMEM((tm, d), BF16), pltpu.VMEM((tm, d), F32)],
        compiler_params=_cparams("parallel", "arbitrary"),
        name="ffn_ple",
    )(h, gpre.reshape(1, d), w1, w2, gpost.reshape(1, d), gple.reshape(1, d), wg, p, wp)


def _pair_tables(nq, last_kt_of, descending):
    qi, kt, first, last = [], [], [], []
    for i in range(nq):
        kts = list(range(last_kt_of(i) + 1))
        if descending:
            kts = kts[::-1]
        for n, t in enumerate(kts):
            qi.append(i)
            kt.append(t)
            first.append(1 if n == 0 else 0)
            last.append(1 if n == len(kts) - 1 else 0)
    mk = lambda v: jnp.asarray(np.asarray(v, dtype=np.int32))
    return mk(qi), mk(kt), mk(first), mk(last)


def _sbt_kernel(qi_ref, kt_ref, first_ref, last_ref, qt_ref, k_ref, vt_ref, o_ref, acc_ref, carry_ref, z_ref, *, tq, tk):
    s = pl.program_id(1)
    n_heads = qt_ref.shape[0] // HEAD_DIM
    past = lax.broadcasted_iota(I32, (tk, tq), 0) < lax.broadcasted_iota(I32, (tk, tq), 1)
    upper = jnp.where(lax.broadcasted_iota(I32, (tk, tk), 1) > lax.broadcasted_iota(I32, (tk, tk), 0), 1.0, 0.0)
    w_aug = jnp.concatenate([upper.astype(BF16), jnp.ones((SUB_BF16, tk), BF16)], axis=0)
    zeros_q = jnp.zeros((HEAD_DIM, tq), BF16)

    def logits(h):
        qh = qt_ref[h * HEAD_DIM:(h + 1) * HEAD_DIM, :]
        qm = jnp.concatenate([qh, zeros_q] if h % 2 == 0 else [zeros_q, qh], axis=0)
        return _dot(k_ref[:, (h // 2) * LANES:(h // 2 + 1) * LANES], qm)

    def sweep(diag):
        def finish(h, zs, later):
            a = jnp.exp(zs - later)
            if diag:
                a = jnp.where(past, a, 0.0)
            rows = slice(h * HEAD_DIM, (h + 1) * HEAD_DIM)
            acc_ref[rows, :] += _dot(vt_ref[rows, :], a.astype(BF16))

        slots = z_ref.shape[0]
        for h in range(min(slots - 1, n_heads)):
            z_ref[h] = logits(h)
        pending = None
        for h in range(n_heads):
            z = z_ref[h % slots]
            if h + slots - 1 < n_heads:
                z_ref[(h + slots - 1) % slots] = logits(h + slots - 1)
            neg_abs = lax.bitcast_convert_type(lax.bitcast_convert_type(z, I32) | INT_MIN, F32)
            sp = jnp.maximum(z, 0.0) + jnp.log(1.0 + jnp.exp(neg_abs))
            if diag:
                sp = jnp.where(past, sp, 0.0)
            res = _dot(w_aug, sp.astype(BF16))
            c = carry_ref[h]
            later = (res[:tk].reshape(tk // SUB, SUB, tq) + c[None]).reshape(tk, tq)
            carry_ref[h] = c + res[tk:tk + SUB]
            if pending is not None:
                finish(*pending)
            pending = (h, z - sp, later)
        finish(*pending)

    @pl.when(first_ref[s] == 1)
    def _():
        acc_ref[...] = jnp.zeros_like(acc_ref)
        carry_ref[...] = jnp.zeros_like(carry_ref)
        sweep(True)

    @pl.when(first_ref[s] == 0)
    def _():
        sweep(False)

    @pl.when(last_ref[s] == 1)
    def _():
        for hp in range(o_ref.shape[1] // LANES):
            o_ref[:, hp * LANES:(hp + 1) * LANES] = acc_ref[hp * LANES:(hp + 1) * LANES, :].T.astype(o_ref.dtype)


def _sbt_attention(qt, k, vt, b, s, d, tq=256):
    nq = s // tq
    tabs = _pair_tables(nq, lambda i: i, descending=True)
    n_heads = d // HEAD_DIM
    grid_spec = pltpu.PrefetchScalarGridSpec(
        num_scalar_prefetch=4,
        grid=(b, int(tabs[0].shape[0])),
        in_specs=[
            pl.BlockSpec((d, tq), lambda bb, ss, qi, kt, fi, la: (0, bb * nq + qi[ss])),
            pl.BlockSpec((tq, d), lambda bb, ss, qi, kt, fi, la: (bb * nq + kt[ss], 0)),
            pl.BlockSpec((d, tq), lambda bb, ss, qi, kt, fi, la: (0, bb * nq + kt[ss])),
        ],
        out_specs=pl.BlockSpec((tq, d), lambda bb, ss, qi, kt, fi, la: (bb * nq + qi[ss], 0)),
        scratch_shapes=[pltpu.VMEM((d, tq), F32), pltpu.VMEM((n_heads, SUB, tq), F32),
                        pltpu.VMEM((4, tq, tq), F32)],
    )
    return pl.pallas_call(
        functools.partial(_sbt_kernel, tq=tq, tk=tq),
        grid_spec=grid_spec,
        out_shape=jax.ShapeDtypeStruct((b * s, d), BF16),
        compiler_params=_cparams("parallel", "arbitrary"),
        name="sbt_attn",
    )(*tabs, qt, k, vt)


def _banded_kernel(q_ref, kp_ref, kc_ref, vp_ref, vc_ref, o_ref, lse_ref, s_ref):
    i = pl.program_id(2)
    tq = q_ref.shape[0]
    n_pairs = q_ref.shape[1] // LANES
    row = lax.broadcasted_iota(I32, (tq, 2 * tq), 0)
    col = lax.broadcasted_iota(I32, (tq, 2 * tq), 1)
    lo_col = jnp.where(i > 0, row, jnp.maximum(row, tq))
    bias = jnp.where(col >= lo_col, jnp.where(col <= row + tq, 0.0, NEG), NEG)
    lo_half = _lane_lt_half((tq, LANES))
    for hp in range(n_pairs):
        sl = slice(hp * LANES, (hp + 1) * LANES)
        qp = q_ref[:, sl] * SCALE
        k2 = jnp.concatenate([kp_ref[:, sl], kc_ref[:, sl]], axis=0)
        for half in range(2):
            qm = jnp.where(lo_half if half == 0 else jnp.logical_not(lo_half), qp, jnp.zeros_like(qp))
            s_ref[2 * hp + half] = _dot_nt(qm, k2)
    for hp in range(n_pairs):
        sl = slice(hp * LANES, (hp + 1) * LANES)
        v2 = jnp.concatenate([vp_ref[:, sl], vc_ref[:, sl]], axis=0)
        outs, lses = [], []
        for half in range(2):
            sc = s_ref[2 * hp + half] + bias
            m = jnp.max(sc, axis=1, keepdims=True)
            p = jnp.exp(sc - m)
            l = jnp.sum(p, axis=1, keepdims=True)
            outs.append(_dot(p.astype(BF16), v2) / l)
            lses.append(jnp.broadcast_to(m + jnp.log(l), (tq, LANES)))
        o_ref[:, sl] = jnp.where(lo_half, outs[0], outs[1])
        lse_ref[:, sl] = jnp.where(lo_half, lses[0], lses[1])


def _banded_attention(proj, b, s, dil, gw):
    ncol = 3
    l = s // dil
    view = proj.reshape(b, l, dil * ncol * gw)
    nblk = l // BAND
    q_map = lambda bb, r, i: (bb, i, r * ncol)
    kc_map = lambda bb, r, i: (bb, i, r * ncol + 1)
    kp_map = lambda bb, r, i: (bb, jnp.maximum(i - 1, 0), r * ncol + 1)
    vc_map = lambda bb, r, i: (bb, i, r * ncol + 2)
    vp_map = lambda bb, r, i: (bb, jnp.maximum(i - 1, 0), r * ncol + 2)
    blk = lambda m: pl.BlockSpec((None, BAND, gw), m)
    out_map = lambda bb, r, i: (bb, i, r)
    o, lse = pl.pallas_call(
        _banded_kernel,
        grid=(b, dil, nblk),
        in_specs=[blk(q_map), blk(kp_map), blk(kc_map), blk(vp_map), blk(vc_map)],
        out_specs=[blk(out_map), blk(out_map)],
        out_shape=[jax.ShapeDtypeStruct((b, l, dil * gw), F32)] * 2,
        scratch_shapes=[pltpu.VMEM((gw // HEAD_DIM, BAND, 2 * BAND), F32)],
        compiler_params=_cparams("parallel", "parallel", "arbitrary"),
        name="banded_attn",
    )(view, view, view, view, view)
    return o.reshape(b * l, dil * gw), lse.reshape(b * l, dil * gw)


def _flash_sweep(qt_ref, k_ref, vt_ref, m_ref, l_ref, acc_ref, s_ref, bias_fn, tk, tq, q_scale):
    n_heads = qt_ref.shape[0] // HEAD_DIM
    zeros_q = jnp.zeros((HEAD_DIM, tq), BF16)
    ones_v = jnp.ones((SUB_BF16, tk), BF16)
    groups = HEAD_DIM // SUB

    def logits(h):
        qh = qt_ref[h * HEAD_DIM:(h + 1) * HEAD_DIM, :]
        if q_scale is not None:
            qh = qh * q_scale
        qh = qh.astype(BF16)
        qm = jnp.concatenate([qh, zeros_q] if h % 2 == 0 else [zeros_q, qh], axis=0)
        return _dot(k_ref[:, (h // 2) * LANES:(h // 2 + 1) * LANES].astype(BF16), qm)

    slots = s_ref.shape[0]
    for h in range(min(slots - 1, n_heads)):
        s_ref[h] = logits(h)
    for h in range(n_heads):
        sc = (s_ref[h % slots] + bias_fn(h)).reshape(tk // SUB, SUB, tq)
        if h + slots - 1 < n_heads:
            s_ref[(h + slots - 1) % slots] = logits(h + slots - 1)
        rows = slice(h * HEAD_DIM, (h + 1) * HEAD_DIM)
        m_prev = m_ref[h]
        m_new = jnp.maximum(m_prev, jnp.max(jnp.max(sc, axis=0), axis=0, keepdims=True))
        p = jnp.exp2(sc - m_new[None]).reshape(tk, tq).astype(BF16)
        alpha = jnp.exp2(m_prev - m_new)
        va = jnp.concatenate([vt_ref[rows, :].astype(BF16), ones_v], axis=0)
        pv = _dot(va, p)
        l_ref[h] = alpha * l_ref[h] + pv[HEAD_DIM:HEAD_DIM + SUB]
        m_ref[h] = m_new
        acc = acc_ref[rows, :].reshape(groups, SUB, tq) * alpha[None] + pv[:HEAD_DIM].reshape(groups, SUB, tq)
        acc_ref[rows, :] = acc.reshape(HEAD_DIM, tq)


def _flash_init(m_ref, l_ref, acc_ref):
    m_ref[...] = jnp.full_like(m_ref, NEG)
    l_ref[...] = jnp.zeros_like(l_ref)
    acc_ref[...] = jnp.zeros_like(acc_ref)


def _flash_finish(o_ref, l_ref, acc_ref, tq):
    groups = HEAD_DIM // SUB
    for hp in range(o_ref.shape[1] // LANES):
        halves = []
        for h in (2 * hp, 2 * hp + 1):
            o = acc_ref[h * HEAD_DIM:(h + 1) * HEAD_DIM, :].reshape(groups, SUB, tq) / l_ref[h][None]
            halves.append(o.reshape(HEAD_DIM, tq))
        o_ref[:, hp * LANES:(hp + 1) * LANES] = jnp.concatenate(halves, axis=0).T.astype(o_ref.dtype)


def _dsa_index_kernel(qit_ref, wit_ref, ki_ref, out_ref, key_ref, *, tq, tk, topk):
    i = pl.program_id(1)
    n_chunks_total = out_ref.shape[0]
    nch = (i * tq + tq + tk - 1) // tk
    q_pos = i * tq + lax.broadcasted_iota(I32, (tk, tq), 1)
    k_row = lax.broadcasted_iota(I32, (tk, tq), 0)
    wi = wit_ref[...] * (IDX_HEADS ** -0.5)
    zeros_q = jnp.zeros((HEAD_DIM, tq), BF16)

    def score_chunk(c, carry):
        ks = ki_ref[pl.ds(pl.multiple_of(c * tk, tk), tk), :].astype(BF16)
        score = jnp.zeros((tk, tq), F32)
        for h in range(IDX_HEADS):
            qh = qit_ref[h * HEAD_DIM:(h + 1) * HEAD_DIM, :].astype(BF16)
            rel = jnp.maximum(_dot(ks, jnp.concatenate([qh, zeros_q], axis=0)), 0.0)
            score = score + wi[h:h + 1, :] * rel
        score = jnp.where(c * tk + k_row <= q_pos, score, -jnp.inf) + 0.0
        bits = lax.bitcast_convert_type(score, I32)
        key_ref[c] = jnp.where(bits < 0, bits ^ 0x7FFFFFFF, bits)
        return carry

    lax.fori_loop(0, nch, score_chunk, 0)

    def count(pred_fn):
        lanes = 4
        def body(c, accs):
            keys = key_ref[c]
            accs = list(accs)
            for g in range(tk // SUB):
                accs[g % lanes] = accs[g % lanes] + jnp.where(pred_fn(keys[g * SUB:(g + 1) * SUB, :]), 1.0, 0.0)
            return tuple(accs)
        accs = lax.fori_loop(0, nch, body, tuple(jnp.zeros((SUB, tq), F32) for _ in range(lanes)))
        acc = (accs[0] + accs[1]) + (accs[2] + accs[3])
        return jnp.sum(acc, axis=0, keepdims=True)

    def bit_step(b, t_u):
        cand_u = t_u | lax.shift_left(jnp.int32(1), 31 - b)
        cand = jnp.broadcast_to(cand_u ^ INT_MIN, (SUB, tq))
        cnt = count(lambda kk: kk >= cand)
        return jnp.where(cnt >= topk, cand_u, t_u)

    t_u = lax.fori_loop(0, 32, bit_step, jnp.zeros((1, tq), I32))
    thr = t_u ^ INT_MIN
    thr_s = jnp.broadcast_to(thr, (SUB, tq))
    need = topk - count(lambda kk: kk > thr_s)
    lt = jnp.where(lax.broadcasted_iota(I32, (tk, tk), 1) < lax.broadcasted_iota(I32, (tk, tk), 0), 1.0, 0.0).astype(BF16)

    def mask_chunk(c, seen):
        keys = key_ref[c]
        eq = jnp.where(keys == thr, 1.0, 0.0)
        rank = _dot(lt, eq.astype(BF16)) + seen
        take = jnp.where(keys > thr, 1.0, jnp.where(rank < need, eq, 0.0))
        take = jnp.where(c * tk + k_row <= q_pos, take, 0.0)
        out_ref[c] = jnp.where(take > 0.5, 0.0, NEG).astype(out_ref.dtype)
        return seen + jnp.sum(eq, axis=0, keepdims=True)

    lax.fori_loop(0, nch, mask_chunk, jnp.zeros((1, tq), F32))

    def fill_chunk(c, carry):
        out_ref[c] = jnp.full((tk, tq), NEG, out_ref.dtype)
        return carry

    lax.fori_loop(nch, n_chunks_total, fill_chunk, 0)


def _dsa_index(qqt, wit, kk, b, s, d, tq=256, tk=512):
    nq, nk = s // tq, s // tk
    n_qi = IDX_HEADS * HEAD_DIM
    return pl.pallas_call(
        functools.partial(_dsa_index_kernel, tq=tq, tk=tk, topk=min(TOPK_TOKENS, s // 4)),
        grid=(b, nq),
        in_specs=[
            pl.BlockSpec((n_qi, tq), lambda bb, i: (d // n_qi, bb * nq + i)),
            pl.BlockSpec((IDX_HEADS, tq), lambda bb, i: (0, bb * nq + i)),
            pl.BlockSpec((s, LANES), lambda bb, i: (bb, d // LANES)),
        ],
        out_specs=pl.BlockSpec((None, None, nk, tk, tq), lambda bb, i: (bb, i, 0, 0, 0)),
        out_shape=jax.ShapeDtypeStruct((b, nq, nk, tk, tq), BF16),
        scratch_shapes=[pltpu.VMEM((nk, tk, tq), I32)],
        compiler_params=_cparams("parallel", "arbitrary"),
        name="dsa_index",
    )(qqt, wit, kk)


def _dsa_attn_kernel(qi_ref, kt_ref, first_ref, last_ref, qt_ref, k_ref, vt_ref, bias_ref, o_ref,
                     m_ref, l_ref, acc_ref, s_ref, *, tq, tk):
    s = pl.program_id(1)

    @pl.when(first_ref[s] == 1)
    def _():
        _flash_init(m_ref, l_ref, acc_ref)

    bias = bias_ref[...].astype(F32)
    _flash_sweep(qt_ref, k_ref, vt_ref, m_ref, l_ref, acc_ref, s_ref, lambda h: bias, tk, tq, None)

    @pl.when(last_ref[s] == 1)
    def _():
        _flash_finish(o_ref, l_ref, acc_ref, tq)


def _flash_specs(b, s, d, tq, tk, tabs, bias_spec, slots, vt_row_block=0):
    n_heads = d // HEAD_DIM
    nq, nk = s // tq, s // tk
    return pltpu.PrefetchScalarGridSpec(
        num_scalar_prefetch=4,
        grid=(b, int(tabs[0].shape[0])),
        in_specs=[
            pl.BlockSpec((d, tq), lambda bb, ss, qi, kt, fi, la: (0, bb * nq + qi[ss])),
            pl.BlockSpec((tk, d), lambda bb, ss, qi, kt, fi, la: (bb * nk + kt[ss], 0)),
            pl.BlockSpec((d, tk), lambda bb, ss, qi, kt, fi, la: (vt_row_block, bb * nk + kt[ss])),
            bias_spec,
        ],
        out_specs=pl.BlockSpec((tq, d), lambda bb, ss, qi, kt, fi, la: (bb * nq + qi[ss], 0)),
        scratch_shapes=[pltpu.VMEM((n_heads, SUB, tq), F32), pltpu.VMEM((n_heads, SUB, tq), F32),
                        pltpu.VMEM((d, tq), F32), pltpu.VMEM((slots, tk, tq), F32)],
    )


def _dsa_attention(qqt, kk, vt, bias, b, s, d, tq=256, tk=512):
    tabs = _pair_tables(s // tq, lambda i: (i * tq + tq - 1) // tk, descending=False)
    bias_spec = pl.BlockSpec((None, None, None, tk, tq), lambda bb, ss, qi, kt, fi, la: (bb, qi[ss], kt[ss], 0, 0))
    return pl.pallas_call(
        functools.partial(_dsa_attn_kernel, tq=tq, tk=tk),
        grid_spec=_flash_specs(b, s, d, tq, tk, tabs, bias_spec, slots=4),
        out_shape=jax.ShapeDtypeStruct((b * s, d), BF16),
        compiler_params=_cparams("parallel", "arbitrary"),
        name="dsa_attn",
    )(*tabs, qqt, kk, vt, bias)


def _block_mean_kernel(k_ref, o_ref):
    rows = k_ref.shape[0]
    d = k_ref.shape[1]
    o_ref[...] = jnp.sum(k_ref[...].reshape(rows // MOBA_BLOCK, MOBA_BLOCK, d), axis=1) * (1.0 / MOBA_BLOCK)


def _block_means(k, b, s, d, rows=2048):
    per = rows // MOBA_BLOCK
    per_b = s // rows
    return pl.pallas_call(
        _block_mean_kernel,
        grid=(b, per_b),
        in_specs=[pl.BlockSpec((rows, d), lambda bb, i: (bb * per_b + i, 0))],
        out_specs=pl.BlockSpec((None, per, d), lambda bb, i: (bb, i, 0)),
        out_shape=jax.ShapeDtypeStruct((b, s // MOBA_BLOCK, d), F32),
        compiler_params=_cparams("parallel", "parallel"),
        name="moba_block_mean",
    )(k)


def _moba_select_kernel(qt_ref, kmt_ref, o_ref, *, nb, n_heads, topk):
    cur = pl.program_id(1)
    tq = qt_ref.shape[1]
    width = nb * n_heads
    gate = _dot_precise(kmt_ref[...], qt_ref[...])
    blk = lax.broadcasted_iota(I32, (width, tq), 0) // n_heads
    gate = jnp.where(blk < cur, gate, -jnp.inf)
    rank = jnp.zeros((width, tq), F32)
    for r in range(1, nb):
        sh = r * n_heads
        other = jnp.concatenate([gate[width - sh:], gate[:width - sh]], axis=0)
        rank = rank + jnp.concatenate([jnp.where(other[:sh] > gate[:sh], 1.0, 0.0),
                                       jnp.where(other[sh:] >= gate[sh:], 1.0, 0.0)], axis=0)
    bias = jnp.where(blk < cur, jnp.where(rank < topk, 0.0, NEG), NEG)
    o_ref[...] = bias.reshape(nb, n_heads, tq)


def _moba_select(qt, kmt, b, s, d):
    nb = s // MOBA_BLOCK
    n_heads = d // HEAD_DIM
    topk = min(MOBA_TOPK, nb - 1)
    return pl.pallas_call(
        functools.partial(_moba_select_kernel, nb=nb, n_heads=n_heads, topk=topk),
        grid=(b, nb),
        in_specs=[
            pl.BlockSpec((d, MOBA_BLOCK), lambda bb, i: (0, bb * nb + i)),
            pl.BlockSpec((None, nb * n_heads, d), lambda bb, i: (bb, 0, 0)),
        ],
        out_specs=pl.BlockSpec((None, nb, n_heads, MOBA_BLOCK), lambda bb, i: (bb, 0, 0, i)),
        out_shape=jax.ShapeDtypeStruct((b, nb, n_heads, s), F32),
        compiler_params=_cparams("parallel", "parallel"),
        name="moba_select",
    )(qt, kmt)


def _moba_attn_kernel(qi_ref, kt_ref, first_ref, last_ref, qt_ref, k_ref, vt_ref, sel_ref, o_ref,
                      m_ref, l_ref, acc_ref, s_ref, *, tq):
    s = pl.program_id(1)

    @pl.when(first_ref[s] == 1)
    def _():
        _flash_init(m_ref, l_ref, acc_ref)

    @pl.when(last_ref[s] == 0)
    def _():
        sel = sel_ref[...]
        _flash_sweep(qt_ref, k_ref, vt_ref, m_ref, l_ref, acc_ref, s_ref, lambda h: sel[h:h + 1, :], tq, tq, QK_SCALE_LOG2)

    @pl.when(last_ref[s] == 1)
    def _():
        k_row = lax.broadcasted_iota(I32, (tq, tq), 0)
        q_col = lax.broadcasted_iota(I32, (tq, tq), 1)
        causal = jnp.where(k_row <= q_col, 0.0, NEG)
        _flash_sweep(qt_ref, k_ref, vt_ref, m_ref, l_ref, acc_ref, s_ref, lambda h: causal, tq, tq, QK_SCALE_LOG2)
        _flash_finish(o_ref, l_ref, acc_ref, tq)


def _moba_attention(qt, k, vt, sel, b, s, d):
    tq = MOBA_BLOCK
    tabs = _pair_tables(s // tq, lambda i: i, descending=False)
    n_heads = d // HEAD_DIM
    bias_spec = pl.BlockSpec((None, None, n_heads, tq), lambda bb, ss, qi, kt, fi, la: (bb, kt[ss], 0, qi[ss]))
    return pl.pallas_call(
        functools.partial(_moba_attn_kernel, tq=tq),
        grid_spec=_flash_specs(b, s, d, tq, tq, tabs, bias_spec, slots=6),
        out_shape=jax.ShapeDtypeStruct((b * s, d), BF16),
        compiler_params=_cparams("parallel", "arbitrary"),
        name="moba_attn",
    )(*tabs, qt, k, vt, sel)


def _rope_lane_tables(positions):
    half = ROT_DIM // 2
    inv_freq = ROPE_THETA ** (-jnp.arange(0, ROT_DIM, 2, dtype=F32) / ROT_DIM)
    ang = positions.astype(F32).reshape(-1, 1) * inv_freq
    cos, sin = jnp.cos(ang), jnp.sin(ang)
    n = cos.shape[0]
    pad = HEAD_DIM - ROT_DIM
    c = jnp.concatenate([cos, cos, jnp.ones((n, pad), F32)], axis=1)
    sa = jnp.concatenate([-sin, jnp.zeros((n, half + pad), F32)], axis=1)
    sb = jnp.concatenate([jnp.zeros((n, half), F32), sin, jnp.zeros((n, pad), F32)], axis=1)
    rep = LANES // HEAD_DIM
    return tuple(jnp.tile(t, (1, rep)) for t in (c, sa, sb))


def _moba_gate_matrix(kmean, n_heads):
    b, nb, d = kmean.shape
    head_of_col = jnp.arange(d) // HEAD_DIM
    onehot = (jnp.arange(n_heads)[:, None] == head_of_col[None, :]).astype(F32)
    return (kmean[:, :, None, :] * onehot[None, None, :, :]).reshape(b, nb * n_heads, d)


def kernel(x, p, positions, w_in_sb, w_out_sb, w_in_dil, w_out_dil, w_in_dsa, w_out_dsa, w_in_moba, w_out_moba,
           g_mix_pre, g_mix_post, g_ffn_pre, g_ffn_post, w_ff_in, w_ff_out, g_ple, w_ple_gate, w_ple):
    b, s, d = x.shape
    depth = p.shape[0]
    n = b * s
    n_heads = d // HEAD_DIM
    rope = _rope_lane_tables(positions)
    h = x.reshape(n, d)
    for i in range(depth):
        mixer, j = i % 4, i // 4
        if mixer == 0:
            w = w_in_sb[j].astype(BF16)
            gp = g_mix_pre[i]
            qt = _project(h, gp, w, BF16, layout="cols", scale_cols=d, scale=SCALE, cols=(0, d))
            k = _project(h, gp, w, BF16, cols=(d, d))
            vt = _project(h, gp, w, BF16, layout="cols", cols=(2 * d, d))
            o = _sbt_attention(qt, k, vt, b, s, d)
            h = _outproj(o, w_out_sb[j].astype(BF16), g_mix_post[i], h)
        elif mixer == 1:
            gw = w_out_dil.shape[1]
            w = w_in_dil[j].astype(BF16)
            outs, lses, dils = [], [], [dil for _, dil in DIL_CONFIGS]
            for g, dil in enumerate(dils):
                proj = _project(h, g_mix_pre[i], w, BF16, rope, rope_cols=2 * gw, cols=(3 * gw * g, 3 * gw),
                                tm=512, tn=3 * gw, layout="dilated", dil=dil)
                o_g, lse_g = _banded_attention(proj, b, s, dil, gw)
                outs.append(o_g)
                lses.append(lse_g)
            h = _dil_outproj(outs, lses, dils, w_out_dil[j].astype(BF16), g_mix_post[i], h)
        elif mixer == 2:
            w = w_in_dsa[j].astype(BF16)
            n_qi = IDX_HEADS * HEAD_DIM
            w_q, w_k = w[:, :d], w[:, d:2 * d]
            w_qi = w[:, 3 * d:3 * d + n_qi]
            w_ki = w[:, 3 * d + n_qi:3 * d + n_qi + HEAD_DIM]
            w_wi = jnp.pad(w[:, 3 * d + n_qi + HEAD_DIM:], ((0, 0), (0, LANES - IDX_HEADS)))
            gp = g_mix_pre[i]
            qqt = _project(h, gp, jnp.concatenate([w_q, w_qi], axis=1), BF16, rope, rope_cols=d + n_qi, layout="cols",
                           scale_cols=d, scale=QK_SCALE_LOG2)
            kk = _project(h, gp, jnp.concatenate([w_k, w_ki, w_ki], axis=1), BF16, rope, rope_cols=d + LANES, tn=3 * LANES)
            vt = _project(h, gp, w, BF16, layout="cols", cols=(2 * d, d))
            wit = _project(h, gp, w_wi, F32, tn=LANES, layout="cols")
            bias = _dsa_index(qqt, wit, kk, b, s, d)
            o = _dsa_attention(qqt, kk, vt, bias, b, s, d)
            h = _outproj(o, w_out_dsa[j].astype(BF16), g_mix_post[i], h)
        else:
            w = w_in_moba[j].astype(BF16)
            gp = g_mix_pre[i]
            qt = _project(h, gp, w, F32, rope, rope_cols=d, layout="cols", cols=(0, d))
            k = _project(h, gp, w, F32, rope, rope_cols=d, cols=(d, d))
            vt = _project(h, gp, w, BF16, layout="cols", cols=(2 * d, d))
            kmt = _moba_gate_matrix(_block_means(k, b, s, d), n_heads)
            sel = _moba_select(qt, kmt, b, s, d)
            o = _moba_attention(qt, k, vt, sel, b, s, d)
            h = _outproj(o, w_out_moba[j].astype(BF16), g_mix_post[i], h)
        h = _ffn_ple(h, g_ffn_pre[i], w_ff_in[i].astype(BF16), w_ff_out[i].astype(BF16), g_ffn_post[i],
                     g_ple[i], w_ple_gate[i].astype(BF16), p[i].reshape(n, -1), w_ple[i].astype(BF16))
    return h.reshape(b, s, d)
```

```python
import functools

import jax
import jax.numpy as jnp
import numpy as np
from jax import lax
from jax.experimental import pallas as pl
from jax.experimental.pallas import tpu as pltpu

F32 = jnp.float32
BF16 = jnp.bfloat16
I32 = jnp.int32

LANES = 128
SUB = 8
SUB_BF16 = 16
HEAD_DIM = 64
HALF = HEAD_DIM
ROT_DIM = HEAD_DIM // 4
ROPE_THETA = 500000.0
EPS = 1e-6
NEG = -1e30
SCALE = HEAD_DIM ** -0.5
QK_SCALE_LOG2 = SCALE * 1.4426950408889634
DIL_CONFIGS = ((128, 1), (512, 4), (2048, 16))
BAND = 128
IDX_HEADS = 8
TOPK_TOKENS = 256
MOBA_BLOCK = 256
MOBA_TOPK = 3
INT_MIN = -2 ** 31
VMEM_LIMIT = 52 * 1024 * 1024


def _cparams(*sem):
    return pltpu.CompilerParams(dimension_semantics=sem, vmem_limit_bytes=VMEM_LIMIT)


def _rms(x, g):
    return x * lax.rsqrt(jnp.mean(x * x, axis=-1, keepdims=True) + EPS) * g


def _dot(a, b):
    return jnp.dot(a, b, preferred_element_type=F32)


def _dot_nt(a, b):
    return lax.dot_general(a, b, (((1,), (1,)), ((), ())), preferred_element_type=F32)


def _split3(a):
    hi = a.astype(BF16)
    lo = (a - hi.astype(F32)).astype(BF16)
    return hi, lo


def _dot_precise(a, b):
    a_hi, a_lo = _split3(a)
    b_hi, b_lo = _split3(b)
    return _dot(a_hi, b_hi) + _dot(a_hi, b_lo) + _dot(a_lo, b_hi)


def _lane_lt_half(shape):
    return lax.broadcasted_iota(I32, shape, 1) < HALF


def _proj_kernel(*refs, rope_cols, rope_all, tn, layout, dil, scale_cols, scale):
    refs = list(refs)
    y_ref = refs.pop() if layout == "dilated" else None
    if rope_cols:
        x_ref, g_ref, w_ref, c_ref, sa_ref, sb_ref, o_ref, xn_ref = refs
    else:
        x_ref, g_ref, w_ref, o_ref, xn_ref = refs
    j = pl.program_id(1)
    tm = x_ref.shape[0]

    @pl.when(j == 0)
    def _():
        xn_ref[...] = _rms(x_ref[...], g_ref[...]).astype(BF16)

    y = _dot(xn_ref[...], w_ref[...])
    if scale_cols:
        y = y * jnp.where(j < scale_cols // tn, scale, 1.0)

    def emit(rope_chunks):
        for u in range(tn // LANES):
            lanes = slice(u * LANES, (u + 1) * LANES)
            yu = y[:, lanes]
            if u < rope_chunks:
                yu = (yu * c_ref[...] + pltpu.roll(yu, LANES - ROT_DIM // 2, 1) * sa_ref[...]
                      + pltpu.roll(yu, ROT_DIM // 2, 1) * sb_ref[...])
            if layout == "rows":
                o_ref[:, lanes] = yu.astype(o_ref.dtype)
            elif layout == "cols":
                o_ref[lanes, :] = yu.T.astype(o_ref.dtype)
            else:
                y_ref[u] = yu
        if layout == "dilated":
            for r in range(dil):
                for u in range(tn // LANES):
                    col = r * tn + u * LANES
                    o_ref[:, col:col + LANES] = y_ref[u, pl.ds(r, tm // dil, stride=dil), :].astype(o_ref.dtype)

    if rope_cols == 0:
        emit(0)
    elif layout == "dilated":
        emit(min(rope_cols, tn) // LANES)
    elif rope_all:
        emit(tn // LANES)
    else:
        rope_blocks = rope_cols // tn

        @pl.when(j < rope_blocks)
        def _():
            emit(tn // LANES)

        @pl.when(j >= rope_blocks)
        def _():
            emit(0)


def _project(h, g, w, out_dtype, rope=None, rope_cols=0, tm=1024, tn=512, layout="rows", dil=1, scale_cols=0, scale=1.0,
             cols=None):
    n, d = h.shape
    col0, nout = cols if cols is not None else (0, w.shape[1])
    assert col0 % tn == 0 and nout % tn == 0
    blk0 = col0 // tn
    in_specs = [
        pl.BlockSpec((tm, d), lambda i, j: (i, 0)),
        pl.BlockSpec((1, d), lambda i, j: (0, 0)),
        pl.BlockSpec((d, tn), lambda i, j: (0, j + blk0)),
    ]
    args = [h, g.reshape(1, d), w]
    if rope_cols:
        assert rope_cols % LANES == 0 and (layout == "dilated" or rope_cols % tn == 0)
        in_specs += [pl.BlockSpec((tm, LANES), lambda i, j: (i, 0))] * 3
        args += list(rope)
    scratch = [pltpu.VMEM((tm, d), BF16)]
    if layout == "rows":
        out_spec = pl.BlockSpec((tm, tn), lambda i, j: (i, j))
        out_shape = (n, nout)
    elif layout == "cols":
        out_spec = pl.BlockSpec((tn, tm), lambda i, j: (j, i))
        out_shape = (nout, n)
    else:
        assert tn == nout and tm % (dil * SUB) == 0
        out_spec = pl.BlockSpec((tm // dil, dil * nout), lambda i, j: (i, 0))
        out_shape = (n // dil, dil * nout)
        scratch.append(pltpu.VMEM((tn // LANES, tm, LANES), F32))
    return pl.pallas_call(
        functools.partial(_proj_kernel, rope_cols=rope_cols, rope_all=rope_cols == nout, tn=tn, layout=layout, dil=dil,
                          scale_cols=scale_cols, scale=scale),
        grid=(n // tm, nout // tn),
        in_specs=in_specs,
        out_specs=out_spec,
        out_shape=jax.ShapeDtypeStruct(out_shape, out_dtype),
        scratch_shapes=scratch,
        compiler_params=_cparams("parallel", "arbitrary"),
        name="proj",
    )(*args)


def _outproj_kernel(o_ref, w_ref, g_ref, h_ref, out_ref):
    y = _dot(o_ref[...], w_ref[...])
    out_ref[...] = h_ref[...] + _rms(y, g_ref[...])


def _outproj(o, w, g, h, tm=512):
    n, d = h.shape
    k = o.shape[1]
    return pl.pallas_call(
        _outproj_kernel,
        grid=(n // tm,),
        in_specs=[
            pl.BlockSpec((tm, k), lambda i: (i, 0)),
            pl.BlockSpec((k, d), lambda i: (0, 0)),
            pl.BlockSpec((1, d), lambda i: (0, 0)),
            pl.BlockSpec((tm, d), lambda i: (i, 0)),
        ],
        out_specs=pl.BlockSpec((tm, d), lambda i: (i, 0)),
        out_shape=jax.ShapeDtypeStruct((n, d), F32),
        compiler_params=_cparams("parallel"),
        name="outproj",
    )(o, w, g.reshape(1, d), h)


def _dil_outproj_kernel(*refs, dils):
    n_g = len(dils)
    o_refs, l_refs = refs[:n_g], refs[n_g:2 * n_g]
    w_ref, g_ref, h_ref, out_ref = refs[2 * n_g:2 * n_g + 4]
    scr = refs[2 * n_g + 4:]
    tm = h_ref.shape[0]
    gw = w_ref.shape[0]

    def natural(ref, scr_ref, dil):
        if dil == 1:
            return ref[...]
        for r in range(dil):
            for u in range(gw // LANES):
                col = r * gw + u * LANES
                scr_ref[u, pl.ds(r, tm // dil, stride=dil), :] = ref[:, col:col + LANES]
        return jnp.concatenate([scr_ref[u] for u in range(gw // LANES)], axis=1)

    os_ = [natural(o_refs[i], scr[i], dils[i]) for i in range(n_g)]
    ls_ = [natural(l_refs[i], scr[n_g + i], dils[i]) for i in range(n_g)]
    m = functools.reduce(jnp.maximum, ls_)
    es = [jnp.exp(l - m) for l in ls_]
    o = sum(e * o for e, o in zip(es, os_)) / sum(es)
    y = _dot(o.astype(BF16), w_ref[...])
    out_ref[...] = h_ref[...] + _rms(y, g_ref[...])


def _dil_outproj(os_, ls_, dils, w, g, h, tm=512):
    n, d = h.shape
    k = w.shape[0]
    tiles = [pl.BlockSpec((tm // dil, dil * k), lambda i: (i, 0)) for dil in dils]
    return pl.pallas_call(
        functools.partial(_dil_outproj_kernel, dils=tuple(dils)),
        grid=(n // tm,),
        in_specs=tiles + tiles + [
            pl.BlockSpec((k, d), lambda i: (0, 0)),
            pl.BlockSpec((1, d), lambda i: (0, 0)),
            pl.BlockSpec((tm, d), lambda i: (i, 0)),
        ],
        out_specs=pl.BlockSpec((tm, d), lambda i: (i, 0)),
        out_shape=jax.ShapeDtypeStruct((n, d), F32),
        scratch_shapes=[pltpu.VMEM((k // LANES, tm, LANES), F32)] * (2 * len(dils)),
        compiler_params=_cparams("parallel"),
        name="dil_outproj",
    )(*os_, *ls_, w, g.reshape(1, d), h)


def _ffn_kernel(h_ref, gpre_ref, w1_ref, w2_ref, gpost_ref, gple_ref, wg_ref, p_ref, wp_ref,
                out_ref, un_ref, acc_ref, *, nf):
    f = pl.program_id(1)

    @pl.when(f == 0)
    def _():
        un_ref[...] = _rms(h_ref[...], gpre_ref[...]).astype(BF16)
        acc_ref[...] = jnp.zeros_like(acc_ref)

    a = _dot(un_ref[...], w1_ref[...])
    a = jnp.square(jnp.maximum(a, 0.0)).astype(BF16)
    acc_ref[...] += _dot(a, w2_ref[...])

    @pl.when(f == nf - 1)
    def _():
        h = h_ref[...] + _rms(acc_ref[...], gpost_ref[...])
        u = _rms(h, gple_ref[...]).astype(BF16)
        gate = 1.0 / (1.0 + jnp.exp(-_dot(u, wg_ref[...])))
        e = _dot(p_ref[...].astype(BF16), wp_ref[...])
        out_ref[...] = h + e * gate


def _ffn_ple(h, gpre, w1, w2, gpost, gple, wg, p, wp, tm=1024, tf=512):
    n, d = h.shape
    dff = w1.shape[1]
    pd = p.shape[1]
    nf = dff // tf
    row = lambda i, f: (i, 0)
    const = lambda i, f: (0, 0)
    return pl.pallas_call(
        functools.partial(_ffn_kernel, nf=nf),
        grid=(n // tm, nf),
        in_specs=[
            pl.BlockSpec((tm, d), row),
            pl.BlockSpec((1, d), const),
            pl.BlockSpec((d, tf), lambda i, f: (0, f)),
            pl.BlockSpec((tf, d), lambda i, f: (f, 0)),
            pl.BlockSpec((1, d), const),
            pl.BlockSpec((1, d), const),
            pl.BlockSpec((d, d), const),
            pl.BlockSpec((tm, pd), row),
            pl.BlockSpec((pd, d), const),
        ],
        out_specs=pl.BlockSpec((tm, d), row),
        out_shape=jax.ShapeDtypeStruct((n, d), F32),
        scratch_shapes=[pltpu.VMEM((tm, d), BF16), pltpu.VMEM((tm, d), F32)],
        compiler_params=_cparams("parallel", "arbitrary"),
        name="ffn_ple",
    )(h, gpre.reshape(1, d), w1, w2, gpost.reshape(1, d), gple.reshape(1, d), wg, p, wp)


def _pair_tables(nq, last_kt_of, descending):
    qi, kt, first, last = [], [], [], []
    for i in range(nq):
        kts = list(range(last_kt_of(i) + 1))
        if descending:
            kts = kts[::-1]
        for n, t in enumerate(kts):
            qi.append(i)
            kt.append(t)
            first.append(1 if n == 0 else 0)
            last.append(1 if n == len(kts) - 1 else 0)
    mk = lambda v: jnp.asarray(np.asarray(v, dtype=np.int32))
    return mk(qi), mk(kt), mk(first), mk(last)


def _sbt_kernel(qi_ref, kt_ref, first_ref, last_ref, qt_ref, k_ref, vt_ref, o_ref, acc_ref, carry_ref, z_ref, zs_ref, later_ref,
                *, tq, tk):
    s = pl.program_id(1)
    n_heads = qt_ref.shape[0] // HEAD_DIM
    past = lax.broadcasted_iota(I32, (tk, tq), 0) < lax.broadcasted_iota(I32, (tk, tq), 1)
    upper = jnp.where(lax.broadcasted_iota(I32, (tk, tk), 1) > lax.broadcasted_iota(I32, (tk, tk), 0), 1.0, 0.0)
    w_aug = jnp.concatenate([upper.astype(BF16), jnp.ones((SUB_BF16, tk), BF16)], axis=0)
    zeros_q = jnp.zeros((HEAD_DIM, tq), BF16)

    def logits(h):
        qh = qt_ref[h * HEAD_DIM:(h + 1) * HEAD_DIM, :]
        qm = jnp.concatenate([qh, zeros_q] if h % 2 == 0 else [zeros_q, qh], axis=0)
        return _dot(k_ref[:, (h // 2) * LANES:(h // 2 + 1) * LANES], qm)

    def sweep(diag):
        def finish(h):
            a = jnp.exp(zs_ref[h % fin_slots] - later_ref[h % fin_slots])
            if diag:
                a = jnp.where(past, a, 0.0)
            rows = slice(h * HEAD_DIM, (h + 1) * HEAD_DIM)
            acc_ref[rows, :] += _dot(vt_ref[rows, :], a.astype(BF16))

        slots = z_ref.shape[0]
        for h in range(min(slots - 1, n_heads)):
            z_ref[h] = logits(h)
        fin_slots = zs_ref.shape[0]
        for h in range(n_heads):
            z = z_ref[h % slots]
            if h + slots - 1 < n_heads:
                z_ref[(h + slots - 1) % slots] = logits(h + slots - 1)
            neg_abs = lax.bitcast_convert_type(lax.bitcast_convert_type(z, I32) | INT_MIN, F32)
            sp = jnp.maximum(z, 0.0) + jnp.log(1.0 + jnp.exp(neg_abs))
            if diag:
                sp = jnp.where(past, sp, 0.0)
            res = _dot(w_aug, sp.astype(BF16))
            c = carry_ref[h]
            later_ref[h % fin_slots] = (res[:tk].reshape(tk // SUB, SUB, tq) + c[None]).reshape(tk, tq)
            zs_ref[h % fin_slots] = z - sp
            carry_ref[h] = c + res[tk:tk + SUB]
            if h >= fin_slots - 1:
                finish(h - (fin_slots - 1))
        for h in range(max(n_heads - (fin_slots - 1), 0), n_heads):
            finish(h)

    @pl.when(first_ref[s] == 1)
    def _():
        acc_ref[...] = jnp.zeros_like(acc_ref)
        carry_ref[...] = jnp.zeros_like(carry_ref)
        sweep(True)

    @pl.when(first_ref[s] == 0)
    def _():
        sweep(False)

    @pl.when(last_ref[s] == 1)
    def _():
        for hp in range(o_ref.shape[1] // LANES):
            o_ref[:, hp * LANES:(hp + 1) * LANES] = acc_ref[hp * LANES:(hp + 1) * LANES, :].T.astype(o_ref.dtype)


def _sbt_attention(qt, k, vt, b, s, d, tq=256):
    nq = s // tq
    tabs = _pair_tables(nq, lambda i: i, descending=True)
    n_heads = d // HEAD_DIM
    grid_spec = pltpu.PrefetchScalarGridSpec(
        num_scalar_prefetch=4,
        grid=(b, int(tabs[0].shape[0])),
        in_specs=[
            pl.BlockSpec((d, tq), lambda bb, ss, qi, kt, fi, la: (0, bb * nq + qi[ss])),
            pl.BlockSpec((tq, d), lambda bb, ss, qi, kt, fi, la: (bb * nq + kt[ss], 0)),
            pl.BlockSpec((d, tq), lambda bb, ss, qi, kt, fi, la: (0, bb * nq + kt[ss])),
        ],
        out_specs=pl.BlockSpec((tq, d), lambda bb, ss, qi, kt, fi, la: (bb * nq + qi[ss], 0)),
        scratch_shapes=[pltpu.VMEM((d, tq), F32), pltpu.VMEM((n_heads, SUB, tq), F32),
                        pltpu.VMEM((4, tq, tq), F32), pltpu.VMEM((3, tq, tq), F32), pltpu.VMEM((3, tq, tq), F32)],
    )
    return pl.pallas_call(
        functools.partial(_sbt_kernel, tq=tq, tk=tq),
        grid_spec=grid_spec,
        out_shape=jax.ShapeDtypeStruct((b * s, d), BF16),
        compiler_params=_cparams("parallel", "arbitrary"),
        name="sbt_attn",
    )(*tabs, qt, k, vt)


def _banded_kernel(q_ref, kp_ref, kc_ref, vp_ref, vc_ref, o_ref, lse_ref, s_ref):
    i = pl.program_id(2)
    tq = q_ref.shape[0]
    n_pairs = q_ref.shape[1] // LANES
    row = lax.broadcasted_iota(I32, (tq, 2 * tq), 0)
    col = lax.broadcasted_iota(I32, (tq, 2 * tq), 1)
    lo_col = jnp.where(i > 0, row, jnp.maximum(row, tq))
    bias = jnp.where(col >= lo_col, jnp.where(col <= row + tq, 0.0, NEG), NEG)
    lo_half = _lane_lt_half((tq, LANES))
    for hp in range(n_pairs):
        sl = slice(hp * LANES, (hp + 1) * LANES)
        qp = q_ref[:, sl] * SCALE
        k2 = jnp.concatenate([kp_ref[:, sl], kc_ref[:, sl]], axis=0)
        for half in range(2):
            qm = jnp.where(lo_half if half == 0 else jnp.logical_not(lo_half), qp, jnp.zeros_like(qp))
            s_ref[2 * hp + half] = _dot_nt(qm, k2)
    for hp in range(n_pairs):
        sl = slice(hp * LANES, (hp + 1) * LANES)
        v2 = jnp.concatenate([vp_ref[:, sl], vc_ref[:, sl]], axis=0)
        outs, lses = [], []
        for half in range(2):
            sc = s_ref[2 * hp + half] + bias
            m = jnp.max(sc, axis=1, keepdims=True)
            p = jnp.exp(sc - m)
            l = jnp.sum(p, axis=1, keepdims=True)
            outs.append(_dot(p.astype(BF16), v2) / l)
            lses.append(jnp.broadcast_to(m + jnp.log(l), (tq, LANES)))
        o_ref[:, sl] = jnp.where(lo_half, outs[0], outs[1])
        lse_ref[:, sl] = jnp.where(lo_half, lses[0], lses[1])


def _banded_attention(proj, b, s, dil, gw):
    ncol = 3
    l = s // dil
    view = proj.reshape(b, l, dil * ncol * gw)
    nblk = l // BAND
    q_map = lambda bb, r, i: (bb, i, r * ncol)
    kc_map = lambda bb, r, i: (bb, i, r * ncol + 1)
    kp_map = lambda bb, r, i: (bb, jnp.maximum(i - 1, 0), r * ncol + 1)
    vc_map = lambda bb, r, i: (bb, i, r * ncol + 2)
    vp_map = lambda bb, r, i: (bb, jnp.maximum(i - 1, 0), r * ncol + 2)
    blk = lambda m: pl.BlockSpec((None, BAND, gw), m)
    out_map = lambda bb, r, i: (bb, i, r)
    o, lse = pl.pallas_call(
        _banded_kernel,
        grid=(b, dil, nblk),
        in_specs=[blk(q_map), blk(kp_map), blk(kc_map), blk(vp_map), blk(vc_map)],
        out_specs=[blk(out_map), blk(out_map)],
        out_shape=[jax.ShapeDtypeStruct((b, l, dil * gw), F32)] * 2,
        scratch_shapes=[pltpu.VMEM((gw // HEAD_DIM, BAND, 2 * BAND), F32)],
        compiler_params=_cparams("parallel", "parallel", "arbitrary"),
        name="banded_attn",
    )(view, view, view, view, view)
    return o.reshape(b * l, dil * gw), lse.reshape(b * l, dil * gw)


def _flash_sweep(qt_ref, k_ref, vt_ref, m_ref, l_ref, acc_ref, s_ref, bias_fn, tk, tq, q_scale):
    n_heads = qt_ref.shape[0] // HEAD_DIM
    zeros_q = jnp.zeros((HEAD_DIM, tq), BF16)
    ones_v = jnp.ones((SUB_BF16, tk), BF16)
    groups = HEAD_DIM // SUB

    def logits(h):
        qh = qt_ref[h * HEAD_DIM:(h + 1) * HEAD_DIM, :]
        if q_scale is not None:
            qh = qh * q_scale
        qh = qh.astype(BF16)
        qm = jnp.concatenate([qh, zeros_q] if h % 2 == 0 else [zeros_q, qh], axis=0)
        return _dot(k_ref[:, (h // 2) * LANES:(h // 2 + 1) * LANES].astype(BF16), qm)

    slots = s_ref.shape[0]
    for h in range(min(slots - 1, n_heads)):
        s_ref[h] = logits(h)
    for h in range(n_heads):
        sc = (s_ref[h % slots] + bias_fn(h)).reshape(tk // SUB, SUB, tq)
        if h + slots - 1 < n_heads:
            s_ref[(h + slots - 1) % slots] = logits(h + slots - 1)
        rows = slice(h * HEAD_DIM, (h + 1) * HEAD_DIM)
        m_prev = m_ref[h]
        m_new = jnp.maximum(m_prev, jnp.max(jnp.max(sc, axis=0), axis=0, keepdims=True))
        p = jnp.exp2(sc - m_new[None]).reshape(tk, tq).astype(BF16)
        alpha = jnp.exp2(m_prev - m_new)
        va = jnp.concatenate([vt_ref[rows, :].astype(BF16), ones_v], axis=0)
        pv = _dot(va, p)
        l_ref[h] = alpha * l_ref[h] + pv[HEAD_DIM:HEAD_DIM + SUB]
        m_ref[h] = m_new
        acc = acc_ref[rows, :].reshape(groups, SUB, tq) * alpha[None] + pv[:HEAD_DIM].reshape(groups, SUB, tq)
        acc_ref[rows, :] = acc.reshape(HEAD_DIM, tq)


def _flash_init(m_ref, l_ref, acc_ref):
    m_ref[...] = jnp.full_like(m_ref, NEG)
    l_ref[...] = jnp.zeros_like(l_ref)
    acc_ref[...] = jnp.zeros_like(acc_ref)


def _flash_finish(o_ref, l_ref, acc_ref, tq):
    groups = HEAD_DIM // SUB
    for hp in range(o_ref.shape[1] // LANES):
        halves = []
        for h in (2 * hp, 2 * hp + 1):
            o = acc_ref[h * HEAD_DIM:(h + 1) * HEAD_DIM, :].reshape(groups, SUB, tq) / l_ref[h][None]
            halves.append(o.reshape(HEAD_DIM, tq))
        o_ref[:, hp * LANES:(hp + 1) * LANES] = jnp.concatenate(halves, axis=0).T.astype(o_ref.dtype)


def _dsa_index_kernel(qit_ref, wit_ref, ki_ref, out_ref, key_ref, *, tq, tk, topk):
    i = pl.program_id(1)
    n_chunks_total = out_ref.shape[0]
    nch = (i * tq + tq + tk - 1) // tk
    q_pos = i * tq + lax.broadcasted_iota(I32, (tk, tq), 1)
    k_row = lax.broadcasted_iota(I32, (tk, tq), 0)
    wi = wit_ref[...] * (IDX_HEADS ** -0.5)
    zeros_q = jnp.zeros((HEAD_DIM, tq), BF16)

    def score_chunk(c, carry):
        ks = ki_ref[pl.ds(pl.multiple_of(c * tk, tk), tk), :].astype(BF16)
        score = jnp.zeros((tk, tq), F32)
        for h in range(IDX_HEADS):
            qh = qit_ref[h * HEAD_DIM:(h + 1) * HEAD_DIM, :].astype(BF16)
            rel = jnp.maximum(_dot(ks, jnp.concatenate([qh, zeros_q], axis=0)), 0.0)
            score = score + wi[h:h + 1, :] * rel
        score = jnp.where(c * tk + k_row <= q_pos, score, -jnp.inf) + 0.0
        bits = lax.bitcast_convert_type(score, I32)
        key_ref[c] = jnp.where(bits < 0, bits ^ 0x7FFFFFFF, bits)
        return carry

    lax.fori_loop(0, nch, score_chunk, 0)

    def count(pred_fn):
        lanes = 4
        def body(c, accs):
            keys = key_ref[c]
            accs = list(accs)
            for g in range(tk // SUB):
                accs[g % lanes] = accs[g % lanes] + jnp.where(pred_fn(keys[g * SUB:(g + 1) * SUB, :]), 1.0, 0.0)
            return tuple(accs)
        accs = lax.fori_loop(0, nch, body, tuple(jnp.zeros((SUB, tq), F32) for _ in range(lanes)))
        acc = (accs[0] + accs[1]) + (accs[2] + accs[3])
        return jnp.sum(acc, axis=0, keepdims=True)

    def bit_step(b, t_u):
        cand_u = t_u | lax.shift_left(jnp.int32(1), 31 - b)
        cand = jnp.broadcast_to(cand_u ^ INT_MIN, (SUB, tq))
        cnt = count(lambda kk: kk >= cand)
        return jnp.where(cnt >= topk, cand_u, t_u)

    t_u = lax.fori_loop(0, 32, bit_step, jnp.zeros((1, tq), I32))
    thr = t_u ^ INT_MIN
    thr_s = jnp.broadcast_to(thr, (SUB, tq))
    need = topk - count(lambda kk: kk > thr_s)
    lt = jnp.where(lax.broadcasted_iota(I32, (tk, tk), 1) < lax.broadcasted_iota(I32, (tk, tk), 0), 1.0, 0.0).astype(BF16)

    def mask_chunk(c, seen):
        keys = key_ref[c]
        eq = jnp.where(keys == thr, 1.0, 0.0)
        rank = _dot(lt, eq.astype(BF16)) + seen
        take = jnp.where(keys > thr, 1.0, jnp.where(rank < need, eq, 0.0))
        take = jnp.where(c * tk + k_row <= q_pos, take, 0.0)
        out_ref[c] = jnp.where(take > 0.5, 0.0, NEG).astype(out_ref.dtype)
        return seen + jnp.sum(eq, axis=0, keepdims=True)

    lax.fori_loop(0, nch, mask_chunk, jnp.zeros((1, tq), F32))

    def fill_chunk(c, carry):
        out_ref[c] = jnp.full((tk, tq), NEG, out_ref.dtype)
        return carry

    lax.fori_loop(nch, n_chunks_total, fill_chunk, 0)


def _dsa_index(qqt, wit, kk, b, s, d, tq=256, tk=512):
    nq, nk = s // tq, s // tk
    n_qi = IDX_HEADS * HEAD_DIM
    return pl.pallas_call(
        functools.partial(_dsa_index_kernel, tq=tq, tk=tk, topk=min(TOPK_TOKENS, s // 4)),
        grid=(b, nq),
        in_specs=[
            pl.BlockSpec((n_qi, tq), lambda bb, i: (d // n_qi, bb * nq + i)),
            pl.BlockSpec((IDX_HEADS, tq), lambda bb, i: (0, bb * nq + i)),
            pl.BlockSpec((s, LANES), lambda bb, i: (bb, d // LANES)),
        ],
        out_specs=pl.BlockSpec((None, None, nk, tk, tq), lambda bb, i: (bb, i, 0, 0, 0)),
        out_shape=jax.ShapeDtypeStruct((b, nq, nk, tk, tq), BF16),
        scratch_shapes=[pltpu.VMEM((nk, tk, tq), I32)],
        compiler_params=_cparams("parallel", "arbitrary"),
        name="dsa_index",
    )(qqt, wit, kk)


def _dsa_attn_kernel(qi_ref, kt_ref, first_ref, last_ref, qt_ref, k_ref, vt_ref, bias_ref, o_ref,
                     m_ref, l_ref, acc_ref, s_ref, *, tq, tk):
    s = pl.program_id(1)

    @pl.when(first_ref[s] == 1)
    def _():
        _flash_init(m_ref, l_ref, acc_ref)

    bias = bias_ref[...].astype(F32)
    _flash_sweep(qt_ref, k_ref, vt_ref, m_ref, l_ref, acc_ref, s_ref, lambda h: bias, tk, tq, None)

    @pl.when(last_ref[s] == 1)
    def _():
        _flash_finish(o_ref, l_ref, acc_ref, tq)


def _flash_specs(b, s, d, tq, tk, tabs, bias_spec, slots, vt_row_block=0):
    n_heads = d // HEAD_DIM
    nq, nk = s // tq, s // tk
    return pltpu.PrefetchScalarGridSpec(
        num_scalar_prefetch=4,
        grid=(b, int(tabs[0].shape[0])),
        in_specs=[
            pl.BlockSpec((d, tq), lambda bb, ss, qi, kt, fi, la: (0, bb * nq + qi[ss])),
            pl.BlockSpec((tk, d), lambda bb, ss, qi, kt, fi, la: (bb * nk + kt[ss], 0)),
            pl.BlockSpec((d, tk), lambda bb, ss, qi, kt, fi, la: (vt_row_block, bb * nk + kt[ss])),
            bias_spec,
        ],
        out_specs=pl.BlockSpec((tq, d), lambda bb, ss, qi, kt, fi, la: (bb * nq + qi[ss], 0)),
        scratch_shapes=[pltpu.VMEM((n_heads, SUB, tq), F32), pltpu.VMEM((n_heads, SUB, tq), F32),
                        pltpu.VMEM((d, tq), F32), pltpu.VMEM((slots, tk, tq), F32)],
    )


def _dsa_attention(qqt, kk, vt, bias, b, s, d, tq=256, tk=512):
    tabs = _pair_tables(s // tq, lambda i: (i * tq + tq - 1) // tk, descending=False)
    bias_spec = pl.BlockSpec((None, None, None, tk, tq), lambda bb, ss, qi, kt, fi, la: (bb, qi[ss], kt[ss], 0, 0))
    return pl.pallas_call(
        functools.partial(_dsa_attn_kernel, tq=tq, tk=tk),
        grid_spec=_flash_specs(b, s, d, tq, tk, tabs, bias_spec, slots=4),
        out_shape=jax.ShapeDtypeStruct((b * s, d), BF16),
        compiler_params=_cparams("parallel", "arbitrary"),
        name="dsa_attn",
    )(*tabs, qqt, kk, vt, bias)


def _block_mean_kernel(k_ref, o_ref):
    rows = k_ref.shape[0]
    d = k_ref.shape[1]
    o_ref[...] = jnp.sum(k_ref[...].reshape(rows // MOBA_BLOCK, MOBA_BLOCK, d), axis=1) * (1.0 / MOBA_BLOCK)


def _block_means(k, b, s, d, rows=2048):
    per = rows // MOBA_BLOCK
    per_b = s // rows
    return pl.pallas_call(
        _block_mean_kernel,
        grid=(b, per_b),
        in_specs=[pl.BlockSpec((rows, d), lambda bb, i: (bb * per_b + i, 0))],
        out_specs=pl.BlockSpec((None, per, d), lambda bb, i: (bb, i, 0)),
        out_shape=jax.ShapeDtypeStruct((b, s // MOBA_BLOCK, d), F32),
        compiler_params=_cparams("parallel", "parallel"),
        name="moba_block_mean",
    )(k)


def _moba_select_kernel(qt_ref, kmt_ref, o_ref, *, nb, n_heads, topk):
    cur = pl.program_id(1)
    tq = qt_ref.shape[1]
    width = nb * n_heads
    gate = _dot_precise(kmt_ref[...], qt_ref[...])
    blk = lax.broadcasted_iota(I32, (width, tq), 0) // n_heads
    gate = jnp.where(blk < cur, gate, -jnp.inf)
    rank = jnp.zeros((width, tq), F32)
    for r in range(1, nb):
        sh = r * n_heads
        other = jnp.concatenate([gate[width - sh:], gate[:width - sh]], axis=0)
        rank = rank + jnp.concatenate([jnp.where(other[:sh] > gate[:sh], 1.0, 0.0),
                                       jnp.where(other[sh:] >= gate[sh:], 1.0, 0.0)], axis=0)
    bias = jnp.where(blk < cur, jnp.where(rank < topk, 0.0, NEG), NEG)
    o_ref[...] = bias.reshape(nb, n_heads, tq)


def _moba_select(qt, kmt, b, s, d):
    nb = s // MOBA_BLOCK
    n_heads = d // HEAD_DIM
    topk = min(MOBA_TOPK, nb - 1)
    return pl.pallas_call(
        functools.partial(_moba_select_kernel, nb=nb, n_heads=n_heads, topk=topk),
        grid=(b, nb),
        in_specs=[
            pl.BlockSpec((d, MOBA_BLOCK), lambda bb, i: (0, bb * nb + i)),
            pl.BlockSpec((None, nb * n_heads, d), lambda bb, i: (bb, 0, 0)),
        ],
        out_specs=pl.BlockSpec((None, nb, n_heads, MOBA_BLOCK), lambda bb, i: (bb, 0, 0, i)),
        out_shape=jax.ShapeDtypeStruct((b, nb, n_heads, s), F32),
        compiler_params=_cparams("parallel", "parallel"),
        name="moba_select",
    )(qt, kmt)


def _moba_attn_kernel(qi_ref, kt_ref, first_ref, last_ref, qt_ref, k_ref, vt_ref, sel_ref, o_ref,
                      m_ref, l_ref, acc_ref, s_ref, *, tq):
    s = pl.program_id(1)

    @pl.when(first_ref[s] == 1)
    def _():
        _flash_init(m_ref, l_ref, acc_ref)

    @pl.when(last_ref[s] == 0)
    def _():
        sel = sel_ref[...]
        _flash_sweep(qt_ref, k_ref, vt_ref, m_ref, l_ref, acc_ref, s_ref, lambda h: sel[h:h + 1, :], tq, tq, QK_SCALE_LOG2)

    @pl.when(last_ref[s] == 1)
    def _():
        k_row = lax.broadcasted_iota(I32, (tq, tq), 0)
        q_col = lax.broadcasted_iota(I32, (tq, tq), 1)
        causal = jnp.where(k_row <= q_col, 0.0, NEG)
        _flash_sweep(qt_ref, k_ref, vt_ref, m_ref, l_ref, acc_ref, s_ref, lambda h: causal, tq, tq, QK_SCALE_LOG2)
        _flash_finish(o_ref, l_ref, acc_ref, tq)


def _moba_attention(qt, k, vt, sel, b, s, d):
    tq = MOBA_BLOCK
    tabs = _pair_tables(s // tq, lambda i: i, descending=False)
    n_heads = d // HEAD_DIM
    bias_spec = pl.BlockSpec((None, None, n_heads, tq), lambda bb, ss, qi, kt, fi, la: (bb, kt[ss], 0, qi[ss]))
    return pl.pallas_call(
        functools.partial(_moba_attn_kernel, tq=tq),
        grid_spec=_flash_specs(b, s, d, tq, tq, tabs, bias_spec, slots=6),
        out_shape=jax.ShapeDtypeStruct((b * s, d), BF16),
        compiler_params=_cparams("parallel", "arbitrary"),
        name="moba_attn",
    )(*tabs, qt, k, vt, sel)


def _rope_lane_tables(positions):
    half = ROT_DIM // 2
    inv_freq = ROPE_THETA ** (-jnp.arange(0, ROT_DIM, 2, dtype=F32) / ROT_DIM)
    ang = positions.astype(F32).reshape(-1, 1) * inv_freq
    cos, sin = jnp.cos(ang), jnp.sin(ang)
    n = cos.shape[0]
    pad = HEAD_DIM - ROT_DIM
    c = jnp.concatenate([cos, cos, jnp.ones((n, pad), F32)], axis=1)
    sa = jnp.concatenate([-sin, jnp.zeros((n, half + pad), F32)], axis=1)
    sb = jnp.concatenate([jnp.zeros((n, half), F32), sin, jnp.zeros((n, pad), F32)], axis=1)
    rep = LANES // HEAD_DIM
    return tuple(jnp.tile(t, (1, rep)) for t in (c, sa, sb))


def _moba_gate_matrix(kmean, n_heads):
    b, nb, d = kmean.shape
    head_of_col = jnp.arange(d) // HEAD_DIM
    onehot = (jnp.arange(n_heads)[:, None] == head_of_col[None, :]).astype(F32)
    return (kmean[:, :, None, :] * onehot[None, None, :, :]).reshape(b, nb * n_heads, d)


def kernel(x, p, positions, w_in_sb, w_out_sb, w_in_dil, w_out_dil, w_in_dsa, w_out_dsa, w_in_moba, w_out_moba,
           g_mix_pre, g_mix_post, g_ffn_pre, g_ffn_post, w_ff_in, w_ff_out, g_ple, w_ple_gate, w_ple):
    b, s, d = x.shape
    depth = p.shape[0]
    n = b * s
    n_heads = d // HEAD_DIM
    rope = _rope_lane_tables(positions)
    h = x.reshape(n, d)
    for i in range(depth):
        mixer, j = i % 4, i // 4
        if mixer == 0:
            w = w_in_sb[j].astype(BF16)
            gp = g_mix_pre[i]
            qt = _project(h, gp, w, BF16, layout="cols", scale_cols=d, scale=SCALE, cols=(0, d))
            k = _project(h, gp, w, BF16, cols=(d, d))
            vt = _project(h, gp, w, BF16, layout="cols", cols=(2 * d, d))
            o = _sbt_attention(qt, k, vt, b, s, d)
            h = _outproj(o, w_out_sb[j].astype(BF16), g_mix_post[i], h)
        elif mixer == 1:
            gw = w_out_dil.shape[1]
            w = w_in_dil[j].astype(BF16)
            outs, lses, dils = [], [], [dil for _, dil in DIL_CONFIGS]
            for g, dil in enumerate(dils):
                proj = _project(h, g_mix_pre[i], w, BF16, rope, rope_cols=2 * gw, cols=(3 * gw * g, 3 * gw),
                                tm=512, tn=3 * gw, layout="dilated", dil=dil)
                o_g, lse_g = _banded_attention(proj, b, s, dil, gw)
                outs.append(o_g)
                lses.append(lse_g)
            h = _dil_outproj(outs, lses, dils, w_out_dil[j].astype(BF16), g_mix_post[i], h)
        elif mixer == 2:
            w = w_in_dsa[j].astype(BF16)
            n_qi = IDX_HEADS * HEAD_DIM
            w_q, w_k = w[:, :d], w[:, d:2 * d]
            w_qi = w[:, 3 * d:3 * d + n_qi]
            w_ki = w[:, 3 * d + n_qi:3 * d + n_qi + HEAD_DIM]
            w_wi = jnp.pad(w[:, 3 * d + n_qi + HEAD_DIM:], ((0, 0), (0, LANES - IDX_HEADS)))
            gp = g_mix_pre[i]
            qqt = _project(h, gp, jnp.concatenate([w_q, w_qi], axis=1), BF16, rope, rope_cols=d + n_qi, layout="cols",
                           scale_cols=d, scale=QK_SCALE_LOG2)
            kk = _project(h, gp, jnp.concatenate([w_k, w_ki, w_ki], axis=1), BF16, rope, rope_cols=d + LANES, tn=3 * LANES)
            vt = _project(h, gp, w, BF16, layout="cols", cols=(2 * d, d))
            wit = _project(h, gp, w_wi, F32, tn=LANES, layout="cols")
            bias = _dsa_index(qqt, wit, kk, b, s, d)
            o = _dsa_attention(qqt, kk, vt, bias, b, s, d)
            h = _outproj(o, w_out_dsa[j].astype(BF16), g_mix_post[i], h)
        else:
            w = w_in_moba[j].astype(BF16)
            gp = g_mix_pre[i]
            qt = _project(h, gp, w, F32, rope, rope_cols=d, layout="cols", cols=(0, d))
            k = _project(h, gp, w, F32, rope, rope_cols=d, cols=(d, d))
            vt = _project(h, gp, w, BF16, layout="cols", cols=(2 * d, d))
            kmt = _moba_gate_matrix(_block_means(k, b, s, d), n_heads)
            sel = _moba_select(qt, kmt, b, s, d)
            o = _moba_attention(qt, k, vt, sel, b, s, d)
            h = _outproj(o, w_out_moba[j].astype(BF16), g_mix_post[i], h)
        h = _ffn_ple(h, g_ffn_pre[i], w_ff_in[i].astype(BF16), w_ff_out[i].astype(BF16), g_ffn_post[i],
                     g_ple[i], w_ple_gate[i].astype(BF16), p[i].reshape(n, -1), w_ple[i].astype(BF16))
    return h.reshape(b, s, d)
```
